```python
import jax, jax.numpy as jnp
from jax import lax
import numpy as np

D_MODEL = 1024
BATCH = 8
SEQ = 4096
DEPTH = 4

HEAD_DIM = 64
ROT_DIM = HEAD_DIM // 4
ROPE_THETA = 500000.0
N_MIXERS = 3
A_HEADS = D_MODEL // HEAD_DIM
A_KV_HEADS = 4
A_WINDOW = 128
WIN_BLK = 128
B_HEADS = D_MODEL // HEAD_DIM
B_GROUPS = ((128, 1), (512, 4), (2048, 16))
C_HEADS = D_MODEL // HEAD_DIM
MOBA_BLOCK = 256
MOBA_TOPK = 3
MOBA_QCHUNK = 16
D_FF = ((8 * D_MODEL + 3 * 256 - 1) // (3 * 256)) * 256
DN_ALPHA = (2 * DEPTH) ** 0.25
DN_BETA = (8 * DEPTH) ** -0.25
LN_EPS = 1e-5
NEG = -1e30

kernel_name = "hybrid_swa_dilated_moba_deepnorm"


def layer_norm(x, g, b):
    xf = x.astype(jnp.float32)
    mu = xf.mean(-1, keepdims=True)
    var = jnp.square(xf - mu).mean(-1, keepdims=True)
    return ((xf - mu) * lax.rsqrt(var + LN_EPS) * g.astype(jnp.float32) + b.astype(jnp.float32)).astype(x.dtype)


def rope_tables(positions):
    inv = ROPE_THETA ** (-jnp.arange(0, ROT_DIM, 2, dtype=jnp.float32) / ROT_DIM)
    ang = positions.astype(jnp.float32)[..., None] * inv
    return jnp.cos(ang), jnp.sin(ang)


def apply_rope(x, cos, sin):
    xf = x.astype(jnp.float32)
    half = ROT_DIM // 2
    x1, x2 = xf[..., :half], xf[..., half:ROT_DIM]
    c, s = cos[:, :, None, :], sin[:, :, None, :]
    return jnp.concatenate([x1 * c - x2 * s, x2 * c + x1 * s, xf[..., ROT_DIM:]], axis=-1).astype(x.dtype)


def block_attention(q, k, v, blk, n_back, with_prev, sink=None):
    B, L, Hq, dh = q.shape
    Hkv = k.shape[2]
    G = Hq // Hkv
    nb = L // blk
    qb = q.astype(jnp.float32).reshape(B, nb, blk, Hkv, G, dh)
    kb = k.astype(jnp.float32).reshape(B, nb, blk, Hkv, dh)
    vb = v.astype(jnp.float32).reshape(B, nb, blk, Hkv, dh)
    if with_prev:
        shift = lambda t: jnp.pad(t, ((0, 0), (1, 0), (0, 0), (0, 0), (0, 0)))[:, :-1]
        kb = jnp.concatenate([shift(kb), kb], axis=2)
        vb = jnp.concatenate([shift(vb), vb], axis=2)
    off = blk if with_prev else 0
    nk = kb.shape[2]
    s = jnp.einsum('bnqhgd,bnkhd->bnhgqk', qb, kb) * (dh ** -0.5)
    diff = jnp.arange(blk)[:, None] + off - jnp.arange(nk)[None, :]
    ok = ((diff >= 0) & (diff <= n_back))[None]
    if with_prev:
        ok = ok & ((jnp.arange(nb)[:, None, None] > 0) | (jnp.arange(nk)[None, None, :] >= blk))
    s = jnp.where(ok[None, :, None, None], s, NEG)
    m = s.max(-1)
    if sink is not None:
        sk = sink.astype(jnp.float32).reshape(1, 1, Hkv, G, 1)
        m = jnp.maximum(m, sk)
    p = jnp.exp(s - m[..., None])
    l = p.sum(-1)
    if sink is not None:
        l = l + jnp.exp(sk - m)
    o = jnp.einsum('bnhgqk,bnkhd->bnqhgd', p, vb) / jnp.transpose(l, (0, 1, 4, 2, 3))[..., None]
    lse = jnp.transpose(m + jnp.log(l), (0, 1, 4, 2, 3)).reshape(B, L, Hq)
    return o.reshape(B, L, Hq, dh), lse


def mixer_a(x, w_qkv, b_qkv, sinks, w_o, b_o, cos, sin):
    B, S, _ = x.shape
    nq, nkv = A_HEADS * HEAD_DIM, A_KV_HEADS * HEAD_DIM
    qkv = x @ w_qkv + b_qkv
    q = apply_rope(qkv[..., :nq].reshape(B, S, A_HEADS, HEAD_DIM), cos, sin)
    k = apply_rope(qkv[..., nq:nq + nkv].reshape(B, S, A_KV_HEADS, HEAD_DIM), cos, sin)
    v = qkv[..., nq + nkv:].reshape(B, S, A_KV_HEADS, HEAD_DIM)
    o, _ = block_attention(q, k, v, WIN_BLK, A_WINDOW - 1, True, sinks)
    return o.reshape(B, S, nq).astype(x.dtype) @ w_o + b_o


def dilated_attention(q, k, v, window, dil):
    B, S, H, dh = q.shape
    span = dil * WIN_BLK
    S_pad = -(-S // span) * span
    L = S_pad // dil

    def to_phase(t):
        t = jnp.pad(t, ((0, 0), (0, S_pad - S), (0, 0), (0, 0)))
        return jnp.transpose(t.reshape(B, L, dil, H, dh), (0, 2, 1, 3, 4)).reshape(B * dil, L, H, dh)

    o, lse = block_attention(to_phase(q), to_phase(k), to_phase(v), WIN_BLK, window // dil, True)
    o = jnp.transpose(o.reshape(B, dil, L, H, dh), (0, 2, 1, 3, 4)).reshape(B, S_pad, H, dh)[:, :S]
    lse = jnp.transpose(lse.reshape(B, dil, L, H), (0, 2, 1, 3)).reshape(B, S_pad, H)[:, :S]
    return o, lse


def mixer_b(x, w_qkv, w_o, cos, sin):
    B, S, _ = x.shape
    qkv = (x @ w_qkv).reshape(B, S, len(B_GROUPS), 3, B_HEADS, HEAD_DIM)
    outs, lses = [], []
    for g, (window, dil) in enumerate(B_GROUPS):
        q = apply_rope(qkv[:, :, g, 0], cos, sin)
        k = apply_rope(qkv[:, :, g, 1], cos, sin)
        o, lse = dilated_attention(q, k, qkv[:, :, g, 2], window, dil)
        outs.append(o)
        lses.append(lse)
    w = jax.nn.softmax(jnp.stack(lses), axis=0)
    o = jnp.sum(w[..., None] * jnp.stack(outs), axis=0)
    return o.reshape(B, S, B_HEADS * HEAD_DIM).astype(x.dtype) @ w_o


def mixer_c(x, w_qkv, w_o, cos, sin):
    B, S, _ = x.shape
    H, dh = C_HEADS, HEAD_DIM
    qkv = (x @ w_qkv).reshape(B, S, 3, H, dh)
    q = apply_rope(qkv[:, :, 0], cos, sin)
    k = apply_rope(qkv[:, :, 1], cos, sin)
    v = qkv[:, :, 2]
    S_pad = -(-S // MOBA_BLOCK) * MOBA_BLOCK
    nblk = S_pad // MOBA_BLOCK
    pad = ((0, 0), (0, S_pad - S), (0, 0), (0, 0))
    q, k, v = jnp.pad(q, pad), jnp.pad(k, pad), jnp.pad(v, pad)
    o_self, lse_self = block_attention(q, k, v, MOBA_BLOCK, MOBA_BLOCK - 1, False)
    qf = q.astype(jnp.float32)
    kblk = k.reshape(B, nblk, MOBA_BLOCK, H, dh)
    vblk = v.reshape(B, nblk, MOBA_BLOCK, H, dh)
    kmean = kblk.astype(jnp.float32).mean(axis=2)
    gate = jnp.einsum('bshd,bnhd->bshn', qf, kmean)
    n_past = jnp.arange(S_pad) // MOBA_BLOCK
    gate = jnp.where((jnp.arange(nblk)[None, :] < n_past[:, None])[None, :, None, :], gate, NEG)
    topk = min(MOBA_TOPK, nblk)
    _, idx = lax.top_k(gate, topk)
    valid = jnp.arange(topk)[None, :] < n_past[:, None]
    kbt = jnp.transpose(kblk, (0, 3, 1, 2, 4))
    vbt = jnp.transpose(vblk, (0, 3, 1, 2, 4))
    nc = S_pad // MOBA_QCHUNK
    q_c = jnp.transpose(qf.reshape(B, nc, MOBA_QCHUNK, H, dh), (1, 0, 2, 3, 4))
    i_c = jnp.transpose(idx.reshape(B, nc, MOBA_QCHUNK, H, topk), (1, 0, 2, 3, 4))
    v_c = valid.reshape(nc, MOBA_QCHUNK, topk)
    bi = jnp.arange(B)[:, None, None, None]
    hi = jnp.arange(H)[None, None, :, None]
    scale = dh ** -0.5

    def chunk(args):
        qc, ic, vc = args
        kg = kbt[bi, hi, ic].astype(jnp.float32)
        vg = vbt[bi, hi, ic].astype(jnp.float32)
        s = jnp.einsum('bqhd,bqhjkd->bqhjk', qc, kg) * scale
        s = jnp.where(vc[None, :, None, :, None], s, NEG)
        m = s.max(axis=(-2, -1))
        p = jnp.exp(s - m[..., None, None])
        l = p.sum(axis=(-2, -1))
        o = jnp.einsum('bqhjk,bqhjkd->bqhd', p, vg) / l[..., None]
        return o, m + jnp.log(l)

    o_sel, lse_sel = lax.map(chunk, (q_c, i_c, v_c))
    o_sel = jnp.transpose(o_sel, (1, 0, 2, 3, 4)).reshape(B, S_pad, H, dh)
    lse_sel = jnp.transpose(lse_sel, (1, 0, 2, 3)).reshape(B, S_pad, H)
    m = jnp.maximum(lse_self, lse_sel)
    w1 = jnp.exp(lse_self - m)
    w2 = jnp.exp(lse_sel - m)
    o = (w1[..., None] * o_self + w2[..., None] * o_sel) / (w1 + w2)[..., None]
    return o[:, :S].reshape(B, S, H * dh).astype(x.dtype) @ w_o


def swiglu(x, w_gate_up, w_down):
    gu = x @ w_gate_up
    gate, up = gu[..., :D_FF], gu[..., D_FF:]
    return (jax.nn.silu(gate) * up) @ w_down


def setup_inputs(seed: int = 0) -> dict:
    key = jax.random.key(seed)
    ks = jax.random.split(key, 16)
    n_a = (DEPTH + 2) // 3
    n_b = (DEPTH + 1) // 3
    n_c = DEPTH // 3
    D = D_MODEL
    a_cols = (A_HEADS + 2 * A_KV_HEADS) * HEAD_DIM
    b_cols = len(B_GROUPS) * 3 * B_HEADS * HEAD_DIM
    c_cols = 3 * C_HEADS * HEAD_DIM
    nrm = lambda k, shape, fan_in, s=1.0: jax.random.normal(k, shape, jnp.float32) * (fan_in ** -0.5) * s
    return {
        "x": jax.random.normal(ks[0], (BATCH, SEQ, D), jnp.float32),
        "positions": jnp.broadcast_to(jnp.arange(SEQ, dtype=jnp.int32), (BATCH, SEQ)),
        "ln_g": 1.0 + 0.02 * jax.random.normal(ks[1], (DEPTH, 2, D), jnp.float32),
        "ln_b": 0.02 * jax.random.normal(ks[2], (DEPTH, 2, D), jnp.float32),
        "a_w_qkv": nrm(ks[3], (n_a, D, a_cols), D),
        "a_b_qkv": 0.02 * jax.random.normal(ks[4], (n_a, a_cols), jnp.float32),
        "a_sinks": 0.5 * jax.random.normal(ks[5], (n_a, A_HEADS), jnp.float32),
        "a_w_o": nrm(ks[6], (n_a, A_HEADS * HEAD_DIM, D), A_HEADS * HEAD_DIM, DN_BETA),
        "a_b_o": 0.02 * jax.random.normal(ks[7], (n_a, D), jnp.float32),
        "b_w_qkv": nrm(ks[8], (n_b, D, b_cols), D),
        "b_w_o": nrm(ks[9], (n_b, B_HEADS * HEAD_DIM, D), B_HEADS * HEAD_DIM, DN_BETA),
        "c_w_qkv": nrm(ks[10], (n_c, D, c_cols), D),
        "c_w_o": nrm(ks[11], (n_c, C_HEADS * HEAD_DIM, D), C_HEADS * HEAD_DIM, DN_BETA),
        "w_gate_up": nrm(ks[12], (DEPTH, D, 2 * D_FF), D),
        "w_down": nrm(ks[13], (DEPTH, D_FF, D), D_FF, DN_BETA),
    }


def reference(x, positions, ln_g, ln_b, a_w_qkv, a_b_qkv, a_sinks, a_w_o, a_b_o,
              b_w_qkv, b_w_o, c_w_qkv, c_w_o, w_gate_up, w_down):
    cos, sin = rope_tables(positions)
    for i in range(DEPTH):
        kind, j = i % N_MIXERS, i // N_MIXERS
        if kind == 0:
            h = mixer_a(x, a_w_qkv[j], a_b_qkv[j], a_sinks[j], a_w_o[j], a_b_o[j], cos, sin)
        elif kind == 1:
            h = mixer_b(x, b_w_qkv[j], b_w_o[j], cos, sin)
        else:
            h = mixer_c(x, c_w_qkv[j], c_w_o[j], cos, sin)
        x = layer_norm(DN_ALPHA * x + h, ln_g[i, 0], ln_b[i, 0])
        x = layer_norm(DN_ALPHA * x + swiglu(x, w_gate_up[i], w_down[i]), ln_g[i, 1], ln_b[i, 1])
    return x
```

```python
import functools

import numpy as np
import jax
import jax.numpy as jnp
from jax import lax
from jax.experimental import pallas as pl
from jax.experimental.pallas import tpu as pltpu

D_MODEL = 1024
N_HEADS = 16
HEAD_DIM = 64
ROT_DIM = HEAD_DIM // 4
ROPE_THETA = 500000.0
A_KV_HEADS = 4
A_WINDOW = 128
BAND_BLK = 128
B_GROUPS = ((128, 1), (512, 4), (2048, 16))
MOBA_BLOCK = 256
MOBA_TOPK = 3
D_FF = 2816
DEPTH = 4
DN_ALPHA = (2 * DEPTH) ** 0.25
LN_EPS = 1e-5
NEG = -1e30

LANES = 128
HEAD_PAIRS = N_HEADS // 2
VMEM_LIMIT = 56 * 1024 * 1024

F32 = jnp.float32
BF16 = jnp.bfloat16
_NT = (((1,), (1,)), ((), ()))
_TN = (((0,), (0,)), ((), ()))


def _params(*sem):
    return pltpu.CompilerParams(dimension_semantics=sem, vmem_limit_bytes=VMEM_LIMIT)


_INV_FREQ = [float(np.float32(ROPE_THETA ** (-(2.0 * i) / ROT_DIM))) for i in range(ROT_DIM // 2)]


def _rope_table_kernel(pos_ref, cos_ref, sp_ref, sm_ref):
    pos = pos_ref[...].astype(F32)
    lane = lax.broadcasted_iota(jnp.int32, (1, LANES), 1)
    c = lane & (HEAD_DIM - 1)
    f = c & (ROT_DIM // 2 - 1)
    inv = jnp.zeros((1, LANES), F32)
    for i, v in enumerate(_INV_FREQ):
        inv = jnp.where(f == i, v, inv)
    ang = pos * inv
    cs, sn = jnp.cos(ang), jnp.sin(ang)
    cos_ref[...] = jnp.where(c < ROT_DIM, cs, 1.0)
    sp_ref[...] = jnp.where((c >= ROT_DIM // 2) & (c < ROT_DIM), sn, 0.0)
    sm_ref[...] = jnp.where(c < ROT_DIM // 2, -sn, 0.0)


def _rope_tables(pos_col):
    t = pos_col.shape[0]
    tm = min(t, 2048)
    spec = pl.BlockSpec((tm, LANES), lambda i: (i, 0))
    shp = jax.ShapeDtypeStruct((t, LANES), F32)
    return pl.pallas_call(
        _rope_table_kernel,
        grid=(t // tm,),
        in_specs=[pl.BlockSpec((tm, 1), lambda i: (i, 0))],
        out_specs=[spec, spec, spec],
        out_shape=[shp, shp, shp],
        compiler_params=_params("parallel"),
        name="rope_tables",
    )(pos_col)


def _proj_kernel(x_ref, w_ref, b_ref, cos_ref, sp_ref, sm_ref, o_ref, *km_ref, tm):
    kind = pl.program_id(1) % 3
    acc = jnp.dot(x_ref[...].astype(BF16), w_ref[...], preferred_element_type=F32) + b_ref[...]

    @pl.when(kind == 2)
    def _():
        o_ref[...] = acc.astype(BF16)

    @pl.when(kind != 2)
    def _():
        scale = jnp.where(kind == 0, HEAD_DIM ** -0.5, 1.0).astype(F32)
        cs, sp, sm = cos_ref[...] * scale, sp_ref[...] * scale, sm_ref[...] * scale
        for c in range(D_MODEL // LANES):
            sl = slice(c * LANES, (c + 1) * LANES)
            a = acc[:, sl]
            y = a * cs + pltpu.roll(a, ROT_DIM // 2, 1) * sp + pltpu.roll(a, LANES - ROT_DIM // 2, 1) * sm
            o_ref[:, sl] = y.astype(BF16)
            if km_ref:
                @pl.when(kind == 1)
                def _():
                    blk = y.reshape(tm // MOBA_BLOCK, MOBA_BLOCK, LANES)
                    km_ref[0][:, 0, sl] = jnp.sum(blk, axis=1) * (1.0 / MOBA_BLOCK)


def _qkv_proj(x, w, b, tables, *, with_kmean=False, tm=1024):
    t, n = x.shape[0], w.shape[1]
    tm = min(tm, t)
    tn = D_MODEL
    tab_spec = pl.BlockSpec((tm, LANES), lambda i, j: (i, 0))
    out_specs = [pl.BlockSpec((tm, tn), lambda i, j: (i, j))]
    out_shape = [jax.ShapeDtypeStruct((t, n), BF16)]
    if with_kmean:
        out_specs.append(pl.BlockSpec((tm // MOBA_BLOCK, 1, D_MODEL), lambda i, j: (i, 0, 0)))
        out_shape.append(jax.ShapeDtypeStruct((t // MOBA_BLOCK, 1, D_MODEL), F32))
    return pl.pallas_call(
        functools.partial(_proj_kernel, tm=tm),
        grid=(t // tm, n // tn),
        in_specs=[
            pl.BlockSpec((tm, D_MODEL), lambda i, j: (i, 0)),
            pl.BlockSpec((D_MODEL, tn), lambda i, j: (0, j)),
            pl.BlockSpec((1, tn), lambda i, j: (0, j)),
            tab_spec, tab_spec, tab_spec,
        ],
        out_specs=out_specs,
        out_shape=out_shape,
        compiler_params=_params("parallel", "arbitrary"),
        name="qkv_proj",
    )(x, w, b, *tables)


def _band_bias(n_back):
    k = np.arange(2 * BAND_BLK)[:, None]
    q = (np.arange(2 * BAND_BLK) % BAND_BLK)[None, :]
    diff = q + BAND_BLK - k
    ok = (diff >= 0) & (diff <= n_back)
    first = ok & (k >= BAND_BLK)
    return jnp.asarray(np.where(np.stack([first, ok]), 0.0, NEG), F32)


def _band_attn_kernel(*refs, with_sink, with_lse):
    refs = list(refs)
    sink_ref = refs.pop(0) if with_sink else None
    q_ref, kp_ref, kc_ref, vp_ref, vc_ref, bias_ref, o_ref = refs[:7]
    lse_ref = refs[7] if with_lse else None
    bias = bias_ref[jnp.minimum(pl.program_id(2), 1)]
    lane = lax.broadcasted_iota(jnp.int32, (BAND_BLK, LANES), 1)
    col = lax.broadcasted_iota(jnp.int32, (1, 2 * BAND_BLK), 1)
    lse_rows = []
    for hp in range(HEAD_PAIRS):
        sl = slice(hp * LANES, (hp + 1) * LANES)
        q = q_ref[0, :, sl]
        zero = jnp.zeros_like(q)
        qs = jnp.concatenate([jnp.where(lane < HEAD_DIM, q, zero), jnp.where(lane >= HEAD_DIM, q, zero)], axis=0)
        kk = jnp.concatenate([kp_ref[0, :, sl], kc_ref[0, :, sl]], axis=0)
        vv = jnp.concatenate([vp_ref[0, :, sl], vc_ref[0, :, sl]], axis=0)
        s = lax.dot_general(kk, qs, _NT, preferred_element_type=F32) + bias
        m = jnp.max(s, axis=0, keepdims=True)
        if with_sink:
            sk = jnp.where(col < BAND_BLK, sink_ref[2 * hp], sink_ref[2 * hp + 1])
            m = jnp.maximum(m, sk)
        p = jnp.exp(s - m)
        l = jnp.sum(p, axis=0, keepdims=True)
        if with_sink:
            l = l + jnp.exp(sk - m)
        pv = lax.dot_general(vv, p.astype(BF16), _TN, preferred_element_type=F32)
        inv = 1.0 / l
        o_t = jnp.concatenate([pv[:HEAD_DIM, :BAND_BLK] * inv[:, :BAND_BLK],
                               pv[HEAD_DIM:, BAND_BLK:] * inv[:, BAND_BLK:]], axis=0)
        o_ref[0, :, sl] = o_t.T.astype(BF16)
        if with_lse:
            lse = m + jnp.log(l)
            lse_rows += [lse[:, :BAND_BLK], lse[:, BAND_BLK:]]
    if with_lse:
        pad = jnp.zeros((LANES - N_HEADS, BAND_BLK), F32)
        lse_ref[0] = jnp.concatenate(lse_rows + [pad], axis=0).T


def _band_attention(qkv, bsz, seq, *, col0, dil, n_back, sinks=None, with_lse=False):
    n = qkv.shape[1]
    nct = n // D_MODEL
    ln = seq // dil
    view = qkv.reshape(bsz, ln, dil * n)
    blk = (1, BAND_BLK, D_MODEL)

    def spec(c, prev):
        if prev:
            return pl.BlockSpec(blk, lambda b, r, i: (b, jnp.maximum(i - 1, 0), r * nct + c))
        return pl.BlockSpec(blk, lambda b, r, i: (b, i, r * nct + c))

    in_specs = [spec(col0, False), spec(col0 + 1, True), spec(col0 + 1, False), spec(col0 + 2, True),
                spec(col0 + 2, False), pl.BlockSpec((2, 2 * BAND_BLK, 2 * BAND_BLK), lambda b, r, i: (0, 0, 0))]
    args = [view, view, view, view, view, _band_bias(n_back)]
    if sinks is not None:
        in_specs.insert(0, pl.BlockSpec(memory_space=pltpu.SMEM))
        args.insert(0, sinks)
    out_specs = [pl.BlockSpec(blk, lambda b, r, i: (b, i, r))]
    out_shape = [jax.ShapeDtypeStruct((bsz, ln, dil * D_MODEL), BF16)]
    if with_lse:
        out_specs.append(pl.BlockSpec((1, BAND_BLK, LANES), lambda b, r, i: (b, i, r)))
        out_shape.append(jax.ShapeDtypeStruct((bsz, ln, dil * LANES), F32))
    outs = pl.pallas_call(
        functools.partial(_band_attn_kernel, with_sink=sinks is not None, with_lse=with_lse),
        grid=(bsz, dil, ln // BAND_BLK),
        in_specs=in_specs,
        out_specs=out_specs,
        out_shape=out_shape,
        compiler_params=_params("parallel", "parallel", "arbitrary"),
        name="band_attn",
    )(*args)
    o = outs[0].reshape(bsz * seq, D_MODEL)
    if with_lse:
        return o, outs[1].reshape(bsz * seq, LANES)
    return o


def _moba_kernel(q_ref, k_ref, v_ref, km_ref, o_ref, sel_ref, *, nblk):
    qb = pl.program_id(2)
    lane = lax.broadcasted_iota(jnp.int32, (MOBA_BLOCK, LANES), 1)
    blk = lax.broadcasted_iota(jnp.int32, (nblk, MOBA_BLOCK), 0)
    blkf = blk.astype(F32)
    kidx = lax.broadcasted_iota(jnp.int32, (MOBA_BLOCK, MOBA_BLOCK), 0)
    qidx = lax.broadcasted_iota(jnp.int32, (MOBA_BLOCK, MOBA_BLOCK), 1)
    causal = jnp.where(kidx <= qidx, 0.0, NEG).astype(F32)
    q = q_ref[0]
    zero = jnp.zeros_like(q)
    diag = pl.multiple_of(qb * MOBA_BLOCK, MOBA_BLOCK)
    halves = []
    for h in range(2):
        qm = jnp.where((lane >= h * HEAD_DIM) & (lane < (h + 1) * HEAD_DIM), q, zero)
        gate = lax.dot_general(km_ref[0], qm.astype(F32), _NT, preferred_element_type=F32,
                               precision=lax.Precision.HIGHEST)
        g = jnp.where(blk < qb, gate, NEG)
        sel = jnp.zeros((nblk, MOBA_BLOCK), F32)
        for r in range(min(MOBA_TOPK, nblk)):
            mx = jnp.max(g, axis=0, keepdims=True)
            idx = jnp.min(jnp.where(g == mx, blkf, float(nblk)), axis=0, keepdims=True)
            hit = blkf == idx
            sel = jnp.where(hit & (r < qb), 1.0, sel)
            g = jnp.where(hit, -jnp.inf, g)
        sel_ref[h] = sel

        def step(j, carry, h=h, qm=qm):
            m, l, acc = carry
            off = pl.multiple_of(j * MOBA_BLOCK, MOBA_BLOCK)
            s = lax.dot_general(k_ref[0, pl.ds(off, MOBA_BLOCK), :], qm, _NT, preferred_element_type=F32)
            chosen = sel_ref[h, pl.ds(j, 1), :] > 0.5
            m_new = jnp.maximum(m, jnp.where(chosen, jnp.max(s, axis=0, keepdims=True), NEG))
            p = jnp.exp(s + jnp.where(chosen, -m_new, NEG))
            alpha = jnp.exp(m - m_new)
            l = alpha * l + jnp.sum(p, axis=0, keepdims=True)
            pv = lax.dot_general(v_ref[0, pl.ds(off, MOBA_BLOCK), :], p.astype(BF16), _TN, preferred_element_type=F32)
            return m_new, l, alpha * acc + pv[h * HEAD_DIM:(h + 1) * HEAD_DIM]

        init = (jnp.full((1, MOBA_BLOCK), NEG, F32), jnp.zeros((1, MOBA_BLOCK), F32),
                jnp.zeros((HEAD_DIM, MOBA_BLOCK), F32))
        m, l, acc = lax.fori_loop(0, qb, step, init)
        s = lax.dot_general(k_ref[0, pl.ds(diag, MOBA_BLOCK), :], qm, _NT, preferred_element_type=F32) + causal
        m_new = jnp.maximum(m, jnp.max(s, axis=0, keepdims=True))
        p = jnp.exp(s - m_new)
        alpha = jnp.exp(m - m_new)
        l = alpha * l + jnp.sum(p, axis=0, keepdims=True)
        pv = lax.dot_general(v_ref[0, pl.ds(diag, MOBA_BLOCK), :], p.astype(BF16), _TN, preferred_element_type=F32)
        acc = alpha * acc + pv[h * HEAD_DIM:(h + 1) * HEAD_DIM]
        halves.append(acc * (1.0 / l))
    o_ref[0] = jnp.concatenate(halves, axis=0).T.astype(BF16)


def _moba_attention(qkv, kmean, bsz, seq):
    nblk = seq // MOBA_BLOCK
    view = qkv.reshape(bsz, seq, 3 * D_MODEL)
    km = kmean.reshape(bsz, nblk, D_MODEL)
    o = pl.pallas_call(
        functools.partial(_moba_kernel, nblk=nblk),
        grid=(bsz, HEAD_PAIRS, nblk),
        in_specs=[
            pl.BlockSpec((1, MOBA_BLOCK, LANES), lambda b, hp, i: (b, i, hp)),
            pl.BlockSpec((1, seq, LANES), lambda b, hp, i: (b, 0, HEAD_PAIRS + hp)),
            pl.BlockSpec((1, seq, LANES), lambda b, hp, i: (b, 0, 2 * HEAD_PAIRS + hp)),
            pl.BlockSpec((1, nblk, LANES), lambda b, hp, i: (b, 0, hp)),
        ],
        out_specs=pl.BlockSpec((1, MOBA_BLOCK, LANES), lambda b, hp, i: (b, i, hp)),
        out_shape=jax.ShapeDtypeStruct((bsz, seq, D_MODEL), BF16),
        scratch_shapes=[pltpu.VMEM((2, nblk, MOBA_BLOCK), F32)],
        compiler_params=_params("parallel", "parallel", "arbitrary"),
        name="moba_attn",
    )(view, view, view, km)
    return o.reshape(bsz * seq, D_MODEL)


def _layer_norm(y, g, b):
    mu = jnp.mean(y, axis=-1, keepdims=True)
    d = y - mu
    var = jnp.mean(d * d, axis=-1, keepdims=True)
    return d * lax.rsqrt(var + LN_EPS) * g + b


def _oproj_ln_kernel(*refs, n_groups):
    o_refs = refs[:n_groups]
    lse_refs = refs[n_groups:2 * n_groups] if n_groups > 1 else ()
    rest = refs[2 * n_groups:] if n_groups > 1 else refs[n_groups:]
    if n_groups > 1:
        e_ref, rest = rest[0], rest[1:]
    w_ref, b_ref, x_ref, g_ref, beta_ref, out_ref = rest
    if n_groups == 1:
        o = o_refs[0][...]
    else:
        lses = [r[...] for r in lse_refs]
        mx = functools.reduce(jnp.maximum, lses)
        es = [jnp.exp(l - mx) for l in lses]
        inv = 1.0 / functools.reduce(lambda a, c: a + c, es)
        o = None
        for e, o_ref in zip(es, o_refs):
            wgt = e * inv
            hi = wgt.astype(BF16)
            lo = (wgt - hi.astype(F32)).astype(BF16)
            wide = (jnp.dot(hi, e_ref[...], preferred_element_type=F32)
                    + jnp.dot(lo, e_ref[...], preferred_element_type=F32))
            term = wide * o_ref[...].astype(F32)
            o = term if o is None else o + term
        o = o.astype(BF16)
    h = jnp.dot(o, w_ref[...], preferred_element_type=F32) + b_ref[...]
    out_ref[...] = _layer_norm(DN_ALPHA * x_ref[...] + h, g_ref[...], beta_ref[...])


def _oproj_ln(os_, lses, w, b, x, g, beta, *, tm=512):
    t = x.shape[0]
    tm = min(tm, t)
    n_groups = len(os_)
    row = lambda i: (i, 0)
    fix = lambda i: (0, 0)
    in_specs = [pl.BlockSpec((tm, D_MODEL), row)] * n_groups
    args = list(os_)
    if n_groups > 1:
        in_specs += [pl.BlockSpec((tm, LANES), row)] * n_groups
        args += list(lses)
        expand = np.zeros((LANES, D_MODEL), np.float32)
        for hd in range(N_HEADS):
            expand[hd, hd * HEAD_DIM:(hd + 1) * HEAD_DIM] = 1.0
        in_specs.append(pl.BlockSpec((LANES, D_MODEL), fix))
        args.append(jnp.asarray(expand, BF16))
    in_specs += [pl.BlockSpec((D_MODEL, D_MODEL), fix), pl.BlockSpec((1, D_MODEL), fix),
                 pl.BlockSpec((tm, D_MODEL), row), pl.BlockSpec((1, D_MODEL), fix), pl.BlockSpec((1, D_MODEL), fix)]
    args += [w, b, x, g, beta]
    return pl.pallas_call(
        functools.partial(_oproj_ln_kernel, n_groups=n_groups),
        grid=(t // tm,),
        in_specs=in_specs,
        out_specs=pl.BlockSpec((tm, D_MODEL), row),
        out_shape=jax.ShapeDtypeStruct((t, D_MODEL), F32),
        compiler_params=_params("parallel"),
        name="oproj_ln",
    )(*args)


def _ffn_ln_kernel(x_ref, wg_ref, wu_ref, wd_ref, g_ref, beta_ref, out_ref):
    x = x_ref[...]
    xb = x.astype(BF16)
    gate = jnp.dot(xb, wg_ref[...], preferred_element_type=F32)
    up = jnp.dot(xb, wu_ref[...], preferred_element_type=F32)
    hid = (gate * jax.nn.sigmoid(gate) * up).astype(BF16)
    y = jnp.dot(hid, wd_ref[...], preferred_element_type=F32)
    out_ref[...] = _layer_norm(DN_ALPHA * x + y, g_ref[...], beta_ref[...])


def _ffn_ln(x, wg, wu, wd, g, beta, *, tm=256):
    t = x.shape[0]
    tm = min(tm, t)
    row = lambda i: (i, 0)
    fix = lambda i: (0, 0)
    resident = dict(pipeline_mode=pl.Buffered(1))
    return pl.pallas_call(
        _ffn_ln_kernel,
        grid=(t // tm,),
        in_specs=[pl.BlockSpec((tm, D_MODEL), row),
                  pl.BlockSpec((D_MODEL, D_FF), fix, **resident),
                  pl.BlockSpec((D_MODEL, D_FF), fix, **resident),
                  pl.BlockSpec((D_FF, D_MODEL), fix, **resident),
                  pl.BlockSpec((1, D_MODEL), fix), pl.BlockSpec((1, D_MODEL), fix)],
        out_specs=pl.BlockSpec((tm, D_MODEL), row),
        out_shape=jax.ShapeDtypeStruct((t, D_MODEL), F32),
        compiler_params=_params("parallel"),
        name="ffn_ln",
    )(x, wg, wu, wd, g, beta)


def _expand_gqa_columns():
    nq = N_HEADS * HEAD_DIM
    nkv = A_KV_HEADS * HEAD_DIM
    grp = N_HEADS // A_KV_HEADS
    head = np.arange(nq) // HEAD_DIM
    kv_col = (head // grp) * HEAD_DIM + np.arange(nq) % HEAD_DIM
    return np.concatenate([np.arange(nq), nq + kv_col, nq + nkv + kv_col])


def kernel(x, positions, ln_g, ln_b, a_w_qkv, a_b_qkv, a_sinks, a_w_o, a_b_o, b_w_qkv, b_w_o, c_w_qkv, c_w_o,
           w_gate_up, w_down):
    bsz, seq, d = x.shape
    assert d == D_MODEL and seq % (B_GROUPS[-1][1] * BAND_BLK) == 0 and seq % MOBA_BLOCK == 0
    t = bsz * seq
    tables = _rope_tables(positions.reshape(t, 1))
    xf = x.reshape(t, d)
    gqa_cols = _expand_gqa_columns()
    zero_bias = jnp.zeros((1, D_MODEL), F32)
    row = lambda v: v.reshape(1, -1)
    for i in range(DEPTH):
        kind, j = i % 3, i // 3
        g0, b0, g1, b1 = row(ln_g[i, 0]), row(ln_b[i, 0]), row(ln_g[i, 1]), row(ln_b[i, 1])
        if kind == 0:
            w = a_w_qkv[j][:, gqa_cols].astype(BF16)
            qkv = _qkv_proj(xf, w, row(a_b_qkv[j][gqa_cols]), tables)[0]
            o = _band_attention(qkv, bsz, seq, col0=0, dil=1, n_back=A_WINDOW - 1, sinks=a_sinks[j])
            xf = _oproj_ln([o], None, a_w_o[j].astype(BF16), row(a_b_o[j]), xf, g0, b0)
        elif kind == 1:
            w = b_w_qkv[j].astype(BF16)
            qkv = _qkv_proj(xf, w, jnp.zeros((1, w.shape[1]), F32), tables)[0]
            os_, lses = [], []
            for gi, (window, dil) in enumerate(B_GROUPS):
                o, lse = _band_attention(qkv, bsz, seq, col0=3 * gi, dil=dil, n_back=window // dil, with_lse=True)
                os_.append(o)
                lses.append(lse)
            xf = _oproj_ln(os_, lses, b_w_o[j].astype(BF16), zero_bias, xf, g0, b0)
        else:
            w = c_w_qkv[j].astype(BF16)
            qkv, kmean = _qkv_proj(xf, w, jnp.zeros((1, w.shape[1]), F32), tables, with_kmean=True)
            o = _moba_attention(qkv, kmean, bsz, seq)
            xf = _oproj_ln([o], None, c_w_o[j].astype(BF16), zero_bias, xf, g0, b0)
        wgu = w_gate_up[i]
        xf = _ffn_ln(xf, wgu[:, :D_FF].astype(BF16), wgu[:, D_FF:].astype(BF16), w_down[i].astype(BF16), g1, b1)
    return xf.reshape(bsz, seq, d)
```

```python
import functools
import math

import numpy as np
import jax
import jax.numpy as jnp
from jax import lax
from jax.experimental import pallas as pl
from jax.experimental.pallas import tpu as pltpu

D_MODEL = 1024
N_HEADS = 16
HEAD_DIM = 64
ROT_DIM = HEAD_DIM // 4
ROPE_THETA = 500000.0
A_KV_HEADS = 4
A_WINDOW = 128
B_GROUPS = ((128, 1), (512, 4), (2048, 16))
MOBA_BLOCK = 256
MOBA_TOPK = 3
D_FF = 2816
DEPTH = 4
DN_ALPHA = (2 * DEPTH) ** 0.25
LN_EPS = 1e-5
NEG = -1e30

LANES = 128
TILE = 128
HEAD_PAIRS = N_HEADS // 2
N_CHUNKS = D_MODEL // LANES
VMEM_LIMIT = 56 * 1024 * 1024
LOG2E = math.log2(math.e)
LN2 = math.log(2.0)
Q_SCALE = HEAD_DIM ** -0.5 * LOG2E

F32 = jnp.float32
BF16 = jnp.bfloat16
_NT = (((1,), (1,)), ((), ()))
_TN = (((0,), (0,)), ((), ()))


def _params(*sem):
    return pltpu.CompilerParams(dimension_semantics=sem, vmem_limit_bytes=VMEM_LIMIT)


_INV_FREQ = [float(np.float32(ROPE_THETA ** (-(2.0 * i) / ROT_DIM))) for i in range(ROT_DIM // 2)]


def _rope_table_kernel(pos_ref, cos_ref, sp_ref, sm_ref):
    pos = pos_ref[...].astype(F32)
    lane = lax.broadcasted_iota(jnp.int32, (1, LANES), 1)
    c = lane & (HEAD_DIM - 1)
    f = c & (ROT_DIM // 2 - 1)
    inv = jnp.zeros((1, LANES), F32)
    for i, v in enumerate(_INV_FREQ):
        inv = jnp.where(f == i, v, inv)
    ang = pos * inv
    cs, sn = jnp.cos(ang), jnp.sin(ang)
    cos_ref[...] = jnp.where(c < ROT_DIM, cs, 1.0)
    sp_ref[...] = jnp.where((c >= ROT_DIM // 2) & (c < ROT_DIM), sn, 0.0)
    sm_ref[...] = jnp.where(c < ROT_DIM // 2, -sn, 0.0)


def _rope_tables(pos_col):
    t = pos_col.shape[0]
    tm = min(t, 2048)
    spec = pl.BlockSpec((tm, LANES), lambda i: (i, 0))
    shp = jax.ShapeDtypeStruct((t, LANES), F32)
    return pl.pallas_call(
        _rope_table_kernel,
        grid=(t // tm,),
        in_specs=[pl.BlockSpec((tm, 1), lambda i: (i, 0))],
        out_specs=[spec, spec, spec],
        out_shape=[shp, shp, shp],
        compiler_params=_params("parallel"),
        name="rope_tables",
    )(pos_col)


PROJ_ROWS = 256


def _proj_kernel(x_ref, w_ref, b_ref, cos_ref, sp_ref, sm_ref, o_ref, *rest, tm, dil, with_kmean):
    kind = pl.program_id(2)
    km_ref = rest[0] if with_kmean else None
    scr_ref = rest[-1] if dil > 1 else None
    scale = jnp.where(kind == 0, Q_SCALE, 1.0).astype(F32)
    is_v = kind == 2
    sub = PROJ_ROWS // dil
    km_rows = []
    for r in range(tm // PROJ_ROWS):
        rows = slice(r * PROJ_ROWS, (r + 1) * PROJ_ROWS)
        acc = jnp.dot(x_ref[0, rows, :], w_ref[...], preferred_element_type=F32) + b_ref[...]
        cs = jnp.where(is_v, 1.0, cos_ref[0, rows, :] * scale)
        sp = jnp.where(is_v, 0.0, sp_ref[0, rows, :] * scale)
        sm = jnp.where(is_v, 0.0, sm_ref[0, rows, :] * scale)
        km_chunks = []
        for c in range(N_CHUNKS):
            sl = slice(c * LANES, (c + 1) * LANES)
            a = acc[:, sl]
            y = a * cs + pltpu.roll(a, ROT_DIM // 2, 1) * sp + pltpu.roll(a, LANES - ROT_DIM // 2, 1) * sm
            if dil == 1:
                o_ref[0, 0, rows, sl] = y.astype(BF16)
            else:
                scr_ref[c] = y
                for ph in range(dil):
                    o_ref[0, ph, r * sub:(r + 1) * sub, sl] = scr_ref.at[c][pl.ds(ph, sub, stride=dil), :].astype(BF16)
            if with_kmean:
                km_chunks.append(jnp.sum(y, axis=0, keepdims=True) * (1.0 / MOBA_BLOCK))
        if with_kmean:
            km_rows.append(jnp.concatenate(km_chunks, axis=1))
    if with_kmean:
        @pl.when(kind == 1)
        def _():
            for r, v in enumerate(km_rows):
                km_ref[0, r] = v


def _qkv_proj(xb, w, b, tables, *, dil=1, with_kmean=False, tm=1024):
    bsz, seq, _ = xb.shape
    tm = min(tm, seq)
    assert PROJ_ROWS == MOBA_BLOCK and tm % PROJ_ROWS == 0 and PROJ_ROWS % (16 * dil) == 0
    tabs = [tb.reshape(bsz, seq, LANES) for tb in tables]
    tab_spec = pl.BlockSpec((1, tm, LANES), lambda bi, i, j: (bi, i, 0))
    out_specs = [pl.BlockSpec((1, dil, tm // dil, D_MODEL), lambda bi, i, j: (bi, 0, i, j))]
    out_shape = [jax.ShapeDtypeStruct((bsz, dil, seq // dil, 3 * D_MODEL), BF16)]
    if with_kmean:
        out_specs.append(pl.BlockSpec((1, tm // MOBA_BLOCK, 1, D_MODEL), lambda bi, i, j: (bi, i, 0, 0)))
        out_shape.append(jax.ShapeDtypeStruct((bsz, seq // MOBA_BLOCK, 1, D_MODEL), F32))
    scratch = [pltpu.VMEM((N_CHUNKS, PROJ_ROWS, LANES), F32)] if dil > 1 else []
    return pl.pallas_call(
        functools.partial(_proj_kernel, tm=tm, dil=dil, with_kmean=with_kmean),
        grid=(bsz, seq // tm, 3),
        in_specs=[
            pl.BlockSpec((1, tm, D_MODEL), lambda bi, i, j: (bi, i, 0)),
            pl.BlockSpec((D_MODEL, D_MODEL), lambda bi, i, j: (0, j)),
            pl.BlockSpec((1, D_MODEL), lambda bi, i, j: (0, j)),
            tab_spec, tab_spec, tab_spec,
        ],
        out_specs=out_specs,
        out_shape=out_shape,
        scratch_shapes=scratch,
        compiler_params=_params("parallel", "parallel", "arbitrary"),
        name="qkv_proj",
    )(xb, w, b, *tabs)


def _band_bias(n_back):
    k = np.arange(TILE)[:, None]
    q = (np.arange(2 * TILE) % TILE)[None, :]
    prev = q + TILE - k <= n_back
    own = k <= q
    return jnp.asarray(np.where(np.stack([prev, np.zeros_like(prev), own]), 0.0, NEG), F32)


def _stack_heads(q):
    lane = lax.broadcasted_iota(jnp.int32, q.shape, 1)
    zero = jnp.zeros_like(q)
    return jnp.concatenate([jnp.where(lane < HEAD_DIM, q, zero), jnp.where(lane >= HEAD_DIM, q, zero)], axis=0)


def _band_attn_kernel(*refs, with_sink, with_lse):
    refs = list(refs)
    sink_ref = refs.pop(0) if with_sink else None
    q_ref, kp_ref, kc_ref, vp_ref, vc_ref, bias_ref, o_ref = refs[:7]
    lse_ref = refs[7] if with_lse else None
    first = (pl.program_id(1) == 0).astype(jnp.int32)
    biases = (bias_ref[first], bias_ref[2])
    col = lax.broadcasted_iota(jnp.int32, (1, 2 * TILE), 1)
    lse_rows = []
    for hp in range(HEAD_PAIRS):
        sl = slice(hp * LANES, (hp + 1) * LANES)
        qs = _stack_heads(q_ref[0, :, sl])
        parts = []
        for k_ref, v_ref, bias in ((kp_ref, vp_ref, biases[0]), (kc_ref, vc_ref, biases[1])):
            s = lax.dot_general(k_ref[0, :, sl], qs, _NT, preferred_element_type=F32) + bias
            m = jnp.max(s, axis=0, keepdims=True)
            p = jnp.exp2(s - m)
            l = jnp.sum(p, axis=0, keepdims=True)
            pv = lax.dot_general(v_ref[0, :, sl], p.astype(BF16), _TN, preferred_element_type=F32)
            parts.append((m, l, pv))
        (m0, l0, pv0), (m1, l1, pv1) = parts
        m = jnp.maximum(m0, m1)
        if with_sink:
            sk = jnp.where(col < TILE, sink_ref[2 * hp], sink_ref[2 * hp + 1]) * LOG2E
            m = jnp.maximum(m, sk)
        a0, a1 = jnp.exp2(m0 - m), jnp.exp2(m1 - m)
        l = a0 * l0 + a1 * l1
        if with_sink:
            l = l + jnp.exp2(sk - m)
        inv = 1.0 / l
        w0, w1 = a0 * inv, a1 * inv
        o_t = jnp.concatenate(
            [pv0[:HEAD_DIM, :TILE] * w0[:, :TILE] + pv1[:HEAD_DIM, :TILE] * w1[:, :TILE],
             pv0[HEAD_DIM:, TILE:] * w0[:, TILE:] + pv1[HEAD_DIM:, TILE:] * w1[:, TILE:]], axis=0)
        o_ref[0, :, sl] = o_t.T.astype(BF16)
        if with_lse:
            lse = (m + jnp.log2(l)) * LN2
            lse_rows += [lse[:, :TILE], lse[:, TILE:]]
    if with_lse:
        pad = jnp.zeros((LANES - N_HEADS, TILE), F32)
        lse_ref[0] = jnp.concatenate(lse_rows + [pad], axis=0).T


def _band_attention(qkv, *, n_back, sinks=None, with_lse=False):
    nseq, ln, _ = qkv.shape
    blk = (1, TILE, D_MODEL)
    cur = lambda c: pl.BlockSpec(blk, lambda s, i: (s, i, c))
    prev = lambda c: pl.BlockSpec(blk, lambda s, i: (s, jnp.maximum(i - 1, 0), c))
    in_specs = [cur(0), prev(1), cur(1), prev(2), cur(2),
                pl.BlockSpec((3, TILE, 2 * TILE), lambda s, i: (0, 0, 0))]
    args = [qkv] * 5 + [_band_bias(n_back)]
    if sinks is not None:
        in_specs.insert(0, pl.BlockSpec(memory_space=pltpu.SMEM))
        args.insert(0, sinks)
    out_specs = [pl.BlockSpec(blk, lambda s, i: (s, i, 0))]
    out_shape = [jax.ShapeDtypeStruct((nseq, ln, D_MODEL), BF16)]
    if with_lse:
        out_specs.append(pl.BlockSpec((1, TILE, LANES), lambda s, i: (s, i, 0)))
        out_shape.append(jax.ShapeDtypeStruct((nseq, ln, LANES), F32))
    return pl.pallas_call(
        functools.partial(_band_attn_kernel, with_sink=sinks is not None, with_lse=with_lse),
        grid=(nseq, ln // TILE),
        in_specs=in_specs,
        out_specs=out_specs,
        out_shape=out_shape,
        compiler_params=_params("parallel", "arbitrary"),
        name="band_attn",
    )(*args)


N_QSUB = MOBA_BLOCK // TILE
N_CHAINS = HEAD_PAIRS * N_QSUB


def _moba_kernel(q_ref, k_ref, v_ref, km_ref, o_ref, qs_ref, sel_ref, m_ref, l_ref, acc_ref, *, nblk):
    qb = pl.program_id(1)
    blk = lax.broadcasted_iota(jnp.int32, (nblk, MOBA_BLOCK), 0)
    blkf = blk.astype(F32)
    lane = lax.broadcasted_iota(jnp.int32, (MOBA_BLOCK, LANES), 1)
    kidx = lax.broadcasted_iota(jnp.int32, (TILE, 2 * TILE), 0)
    qidx = lax.broadcasted_iota(jnp.int32, (TILE, 2 * TILE), 1) & (TILE - 1)
    causal = jnp.where(kidx <= qidx, 0.0, NEG).astype(F32)

    for hp in range(HEAD_PAIRS):
        sl = slice(hp * LANES, (hp + 1) * LANES)
        q = q_ref[0, :, sl]
        zero = jnp.zeros_like(q)
        sels = []
        for h in range(2):
            qm = jnp.where((lane >= h * HEAD_DIM) & (lane < (h + 1) * HEAD_DIM), q, zero)
            gate = lax.dot_general(km_ref[0, :, sl], qm.astype(F32), _NT, preferred_element_type=F32,
                                   precision=lax.Precision.HIGHEST)
            g = jnp.where(blk < qb, gate, NEG)
            sel = jnp.zeros((nblk, MOBA_BLOCK), F32)
            for r in range(min(MOBA_TOPK, nblk)):
                mx = jnp.max(g, axis=0, keepdims=True)
                idx = jnp.min(jnp.where(g == mx, blkf, float(nblk)), axis=0, keepdims=True)
                hit = blkf == idx
                sel = jnp.where(hit & (r < qb), 1.0, sel)
                g = jnp.where(hit, -jnp.inf, g)
            sels.append(sel)
        for qs in range(N_QSUB):
            c = hp * N_QSUB + qs
            rows = slice(qs * TILE, (qs + 1) * TILE)
            qs_ref[c] = _stack_heads(q[rows])
            sel_ref[c] = jnp.concatenate([sels[0][:, rows], sels[1][:, rows]], axis=1)
    m_ref[...] = jnp.full(m_ref.shape, NEG, F32)
    l_ref[...] = jnp.zeros(l_ref.shape, F32)
    acc_ref[...] = jnp.zeros(acc_ref.shape, F32)

    def update(c, hp, off, bias_row, bias_tile):
        sl = slice(hp * LANES, (hp + 1) * LANES)
        s = lax.dot_general(k_ref[0, pl.ds(off, TILE), sl], qs_ref[c], _NT, preferred_element_type=F32)
        if bias_tile is not None:
            s = s + bias_tile
        mj = jnp.max(s, axis=0, keepdims=True)
        m_old = m_ref[c]
        if bias_row is None:
            m_new = jnp.maximum(m_old, mj)
            p = jnp.exp2(s - m_new)
        else:
            m_new = jnp.maximum(m_old, jnp.where(bias_row, mj, NEG))
            p = jnp.exp2(s + jnp.where(bias_row, -m_new, NEG))
        alpha = jnp.exp2(m_old - m_new)
        m_ref[c] = m_new
        l_ref[c] = alpha * l_ref[c] + jnp.sum(p, axis=0, keepdims=True)
        pv = lax.dot_general(v_ref[0, pl.ds(off, TILE), sl], p.astype(BF16), _TN, preferred_element_type=F32)
        acc_ref[c] = alpha * acc_ref[c] + jnp.concatenate([pv[:HEAD_DIM, :TILE], pv[HEAD_DIM:, TILE:]], axis=1)

    def past(jt, carry):
        off = pl.multiple_of(jt * TILE, TILE)
        j = jt // N_QSUB
        for c in range(N_CHAINS):
            update(c, c // N_QSUB, off, sel_ref[c, pl.ds(j, 1), :] > 0.5, None)
        return carry

    lax.fori_loop(0, qb * N_QSUB, past, 0)

    base = pl.multiple_of(qb * MOBA_BLOCK, MOBA_BLOCK)
    for c in range(N_CHAINS):
        hp, qs = c // N_QSUB, c % N_QSUB
        for kt in range(qs + 1):
            update(c, hp, pl.multiple_of(base + kt * TILE, TILE), None, causal if kt == qs else None)

    for c in range(N_CHAINS):
        hp, qs = c // N_QSUB, c % N_QSUB
        o = acc_ref[c] * (1.0 / l_ref[c])
        o_t = jnp.concatenate([o[:, :TILE], o[:, TILE:]], axis=0)
        o_ref[0, qs * TILE:(qs + 1) * TILE, hp * LANES:(hp + 1) * LANES] = o_t.T.astype(BF16)


def _moba_attention(qkv, kmean):
    bsz, seq, _ = qkv.shape
    nblk = seq // MOBA_BLOCK
    whole = dict(pipeline_mode=pl.Buffered(1))
    return pl.pallas_call(
        functools.partial(_moba_kernel, nblk=nblk),
        grid=(bsz, nblk),
        in_specs=[
            pl.BlockSpec((1, MOBA_BLOCK, D_MODEL), lambda b, i: (b, i, 0)),
            pl.BlockSpec((1, seq, D_MODEL), lambda b, i: (b, 0, 1), **whole),
            pl.BlockSpec((1, seq, D_MODEL), lambda b, i: (b, 0, 2), **whole),
            pl.BlockSpec((1, nblk, D_MODEL), lambda b, i: (b, 0, 0)),
        ],
        out_specs=pl.BlockSpec((1, MOBA_BLOCK, D_MODEL), lambda b, i: (b, i, 0)),
        out_shape=jax.ShapeDtypeStruct((bsz, seq, D_MODEL), BF16),
        scratch_shapes=[pltpu.VMEM((N_CHAINS, 2 * TILE, LANES), BF16),
                        pltpu.VMEM((N_CHAINS, nblk, 2 * TILE), F32),
                        pltpu.VMEM((N_CHAINS, 1, 2 * TILE), F32),
                        pltpu.VMEM((N_CHAINS, 1, 2 * TILE), F32),
                        pltpu.VMEM((N_CHAINS, HEAD_DIM, 2 * TILE), F32)],
        compiler_params=_params("parallel", "arbitrary"),
        name="moba_attn",
    )(qkv, qkv, qkv, kmean)


def _layer_norm(y, g, b):
    mu = jnp.mean(y, axis=-1, keepdims=True)
    d = y - mu
    var = jnp.mean(d * d, axis=-1, keepdims=True)
    return d * lax.rsqrt(var + LN_EPS) * g + b


def _oproj_ln_kernel(*refs, dils, tm):
    n = len(dils)
    o_refs, refs = refs[:n], refs[n:]
    if n > 1:
        lse_refs, e_ref, refs = refs[:n], refs[n], refs[n + 1:]
    w_ref, b_ref, x_ref, g_ref, beta_ref, out_ref, outb_ref = refs[:7]
    scr = refs[7:]
    if n == 1:
        o = o_refs[0][0, 0]
    else:
        o_scr, lse_scr = scr
        lses = []
        for gi, d in enumerate(dils):
            if d == 1:
                lses.append(lse_refs[gi][0, 0])
            else:
                for ph in range(d):
                    lse_scr.at[gi][pl.ds(ph, tm // d, stride=d), :] = lse_refs[gi][0, ph]
                lses.append(lse_scr[gi])
        mx = functools.reduce(jnp.maximum, lses)
        es = [jnp.exp(l - mx) for l in lses]
        inv = 1.0 / functools.reduce(lambda a, c: a + c, es)
        chunks = [None] * N_CHUNKS
        for gi, d in enumerate(dils):
            wgt = es[gi] * inv
            hi = wgt.astype(BF16)
            lo = (wgt - hi.astype(F32)).astype(BF16)
            wide = (jnp.dot(hi, e_ref[...], preferred_element_type=F32)
                    + jnp.dot(lo, e_ref[...], preferred_element_type=F32))
            for c in range(N_CHUNKS):
                sl = slice(c * LANES, (c + 1) * LANES)
                if d == 1:
                    og = o_refs[gi][0, 0, :, sl].astype(F32)
                else:
                    for ph in range(d):
                        o_scr.at[gi, c][pl.ds(ph, tm // d, stride=d), :] = o_refs[gi][0, ph, :, sl].astype(F32)
                    og = o_scr[gi, c]
                term = wide[:, sl] * og
                chunks[c] = term if chunks[c] is None else chunks[c] + term
        o = jnp.concatenate(chunks, axis=1).astype(BF16)
    h = jnp.dot(o, w_ref[...], preferred_element_type=F32) + b_ref[...]
    y = _layer_norm(DN_ALPHA * x_ref[0] + h, g_ref[...], beta_ref[...])
    out_ref[0] = y
    outb_ref[0] = y.astype(BF16)


def _oproj_ln(os_, lses, dils, w, b, x, g, beta, *, tm=512):
    bsz, seq, _ = x.shape
    tm = min(tm, seq)
    n = len(os_)
    tok = lambda bi, i: (bi, i, 0)
    ph = lambda bi, i: (bi, 0, i, 0)
    fix = lambda bi, i: (0, 0)
    in_specs = [pl.BlockSpec((1, d, tm // d, D_MODEL), ph) for d in dils]
    args = list(os_)
    scratch = []
    if n > 1:
        in_specs += [pl.BlockSpec((1, d, tm // d, LANES), ph) for d in dils]
        args += list(lses)
        expand = np.zeros((LANES, D_MODEL), np.float32)
        for hd in range(N_HEADS):
            expand[hd, hd * HEAD_DIM:(hd + 1) * HEAD_DIM] = 1.0
        in_specs.append(pl.BlockSpec((LANES, D_MODEL), fix))
        args.append(jnp.asarray(expand, BF16))
        scratch = [pltpu.VMEM((n, N_CHUNKS, tm, LANES), F32), pltpu.VMEM((n, tm, LANES), F32)]
    in_specs += [pl.BlockSpec((D_MODEL, D_MODEL), fix), pl.BlockSpec((1, D_MODEL), fix),
                 pl.BlockSpec((1, tm, D_MODEL), tok), pl.BlockSpec((1, D_MODEL), fix), pl.BlockSpec((1, D_MODEL), fix)]
    args += [w, b, x, g, beta]
    return pl.pallas_call(
        functools.partial(_oproj_ln_kernel, dils=tuple(dils), tm=tm),
        grid=(bsz, seq // tm),
        in_specs=in_specs,
        out_specs=[pl.BlockSpec((1, tm, D_MODEL), tok), pl.BlockSpec((1, tm, D_MODEL), tok)],
        out_shape=[jax.ShapeDtypeStruct((bsz, seq, D_MODEL), F32), jax.ShapeDtypeStruct((bsz, seq, D_MODEL), BF16)],
        scratch_shapes=scratch,
        compiler_params=_params("parallel", "parallel"),
        name="oproj_ln",
    )(*args)


def _ffn_ln_kernel(x_ref, xb_ref, wg_ref, wu_ref, wd_ref, g_ref, beta_ref, out_ref, outb_ref):
    xb = xb_ref[...]
    gate = jnp.dot(xb, wg_ref[...], preferred_element_type=F32)
    up = jnp.dot(xb, wu_ref[...], preferred_element_type=F32)
    hid = (gate * jax.nn.sigmoid(gate) * up).astype(BF16)
    y = jnp.dot(hid, wd_ref[...], preferred_element_type=F32)
    y = _layer_norm(DN_ALPHA * x_ref[...] + y, g_ref[...], beta_ref[...])
    out_ref[...] = y
    outb_ref[...] = y.astype(BF16)


def _ffn_ln(x, xb, wg, wu, wd, g, beta, *, tm=256):
    t = x.shape[0]
    tm = min(tm, t)
    row = lambda i: (i, 0)
    fix = lambda i: (0, 0)
    resident = dict(pipeline_mode=pl.Buffered(1))
    return pl.pallas_call(
        _ffn_ln_kernel,
        grid=(t // tm,),
        in_specs=[pl.BlockSpec((tm, D_MODEL), row), pl.BlockSpec((tm, D_MODEL), row),
                  pl.BlockSpec((D_MODEL, D_FF), fix, **resident),
                  pl.BlockSpec((D_MODEL, D_FF), fix, **resident),
                  pl.BlockSpec((D_FF, D_MODEL), fix, **resident),
                  pl.BlockSpec((1, D_MODEL), fix), pl.BlockSpec((1, D_MODEL), fix)],
        out_specs=[pl.BlockSpec((tm, D_MODEL), row), pl.BlockSpec((tm, D_MODEL), row)],
        out_shape=[jax.ShapeDtypeStruct((t, D_MODEL), F32), jax.ShapeDtypeStruct((t, D_MODEL), BF16)],
        compiler_params=_params("parallel"),
        name="ffn_ln",
    )(x, xb, wg, wu, wd, g, beta)


def _expand_gqa_columns():
    nq = N_HEADS * HEAD_DIM
    nkv = A_KV_HEADS * HEAD_DIM
    grp = N_HEADS // A_KV_HEADS
    head = np.arange(nq) // HEAD_DIM
    kv_col = (head // grp) * HEAD_DIM + np.arange(nq) % HEAD_DIM
    return np.concatenate([np.arange(nq), nq + kv_col, nq + nkv + kv_col])


def kernel(x, positions, ln_g, ln_b, a_w_qkv, a_b_qkv, a_sinks, a_w_o, a_b_o, b_w_qkv, b_w_o, c_w_qkv, c_w_o,
           w_gate_up, w_down):
    bsz, seq, d = x.shape
    assert d == D_MODEL and seq % (B_GROUPS[-1][1] * TILE) == 0 and seq % MOBA_BLOCK == 0
    t = bsz * seq
    tables = _rope_tables(positions.reshape(t, 1))
    xf, xb = x, x.astype(BF16)
    gqa_cols = _expand_gqa_columns()
    zero_bias = jnp.zeros((1, D_MODEL), F32)
    zero_bias3 = jnp.zeros((1, 3 * D_MODEL), F32)
    row = lambda v: v.reshape(1, -1)
    for i in range(DEPTH):
        kind, j = i % 3, i // 3
        g0, b0, g1, b1 = row(ln_g[i, 0]), row(ln_b[i, 0]), row(ln_g[i, 1]), row(ln_b[i, 1])
        if kind == 0:
            w = a_w_qkv[j][:, gqa_cols].astype(BF16)
            qkv = _qkv_proj(xb, w, row(a_b_qkv[j][gqa_cols]), tables)[0]
            o = _band_attention(qkv[:, 0], n_back=A_WINDOW - 1, sinks=a_sinks[j])[0]
            xf, xb = _oproj_ln([o[:, None]], None, (1,), a_w_o[j].astype(BF16), row(a_b_o[j]), xf, g0, b0)
        elif kind == 1:
            w = b_w_qkv[j].astype(BF16)
            os_, lses, dils = [], [], []
            for gi, (window, dil) in enumerate(B_GROUPS):
                cols = slice(3 * gi * D_MODEL, 3 * (gi + 1) * D_MODEL)
                qkv = _qkv_proj(xb, w[:, cols], zero_bias3, tables, dil=dil)[0]
                ln = seq // dil
                o, lse = _band_attention(qkv.reshape(bsz * dil, ln, 3 * D_MODEL), n_back=window // dil, with_lse=True)
                os_.append(o.reshape(bsz, dil, ln, D_MODEL))
                lses.append(lse.reshape(bsz, dil, ln, LANES))
                dils.append(dil)
            xf, xb = _oproj_ln(os_, lses, dils, b_w_o[j].astype(BF16), zero_bias, xf, g0, b0)
        else:
            w = c_w_qkv[j].astype(BF16)
            qkv, kmean = _qkv_proj(xb, w, zero_bias3, tables, with_kmean=True)
            o = _moba_attention(qkv[:, 0], kmean[:, :, 0])
            xf, xb = _oproj_ln([o[:, None]], None, (1,), c_w_o[j].astype(BF16), zero_bias, xf, g0, b0)
        wgu = w_gate_up[i]
        y, yb = _ffn_ln(xf.reshape(t, d), xb.reshape(t, d), wgu[:, :D_FF].astype(BF16), wgu[:, D_FF:].astype(BF16),
                        w_down[i].astype(BF16), g1, b1)
        xf, xb = y.reshape(bsz, seq, d), yb.reshape(bsz, seq, d)
    return xf
```

```python
import functools
import math

import numpy as np
import jax
import jax.numpy as jnp
from jax import lax
from jax.experimental import pallas as pl
from jax.experimental.pallas import tpu as pltpu

D_MODEL = 1024
N_HEADS = 16
HEAD_DIM = 64
ROT_DIM = HEAD_DIM // 4
ROPE_THETA = 500000.0
A_KV_HEADS = 4
A_WINDOW = 128
B_GROUPS = ((128, 1), (512, 4), (2048, 16))
MOBA_BLOCK = 256
MOBA_TOPK = 3
D_FF = 2816
DEPTH = 4
DN_ALPHA = (2 * DEPTH) ** 0.25
LN_EPS = 1e-5
NEG = -1e30

LANES = 128
TILE = 128
HEAD_PAIRS = N_HEADS // 2
N_CHUNKS = D_MODEL // LANES
VMEM_LIMIT = 56 * 1024 * 1024
LOG2E = math.log2(math.e)
LN2 = math.log(2.0)
Q_SCALE = HEAD_DIM ** -0.5 * LOG2E

F32 = jnp.float32
BF16 = jnp.bfloat16
_NT = (((1,), (1,)), ((), ()))
_TN = (((0,), (0,)), ((), ()))


def _params(*sem):
    return pltpu.CompilerParams(dimension_semantics=sem, vmem_limit_bytes=VMEM_LIMIT)


_INV_FREQ = [float(np.float32(ROPE_THETA ** (-(2.0 * i) / ROT_DIM))) for i in range(ROT_DIM // 2)]


def _rope_table_kernel(pos_ref, cos_ref, sp_ref, sm_ref):
    pos = pos_ref[...].astype(F32)
    lane = lax.broadcasted_iota(jnp.int32, (1, LANES), 1)
    c = lane & (HEAD_DIM - 1)
    f = c & (ROT_DIM // 2 - 1)
    inv = jnp.zeros((1, LANES), F32)
    for i, v in enumerate(_INV_FREQ):
        inv = jnp.where(f == i, v, inv)
    ang = pos * inv
    cs, sn = jnp.cos(ang), jnp.sin(ang)
    cos_ref[...] = jnp.where(c < ROT_DIM, cs, 1.0)
    sp_ref[...] = jnp.where((c >= ROT_DIM // 2) & (c < ROT_DIM), sn, 0.0)
    sm_ref[...] = jnp.where(c < ROT_DIM // 2, -sn, 0.0)


def _rope_tables(pos_col):
    t = pos_col.shape[0]
    tm = min(t, 2048)
    spec = pl.BlockSpec((tm, LANES), lambda i: (i, 0))
    shp = jax.ShapeDtypeStruct((t, LANES), F32)
    return pl.pallas_call(
        _rope_table_kernel,
        grid=(t // tm,),
        in_specs=[pl.BlockSpec((tm, 1), lambda i: (i, 0))],
        out_specs=[spec, spec, spec],
        out_shape=[shp, shp, shp],
        compiler_params=_params("parallel"),
        name="rope_tables",
    )(pos_col)


PROJ_ROWS = 256


def _proj_kernel(x_ref, w_ref, b_ref, cos_ref, sp_ref, sm_ref, *rest, tm, dil, with_kmean):
    rest = list(rest)
    perm_ref = rest.pop(0) if dil > 1 else None
    qk_ref = rest.pop(0)
    vt_ref = rest.pop(0) if dil == 1 else None
    km_ref = rest.pop(0) if with_kmean else None
    sub = PROJ_ROWS // dil
    for r in range(tm // PROJ_ROWS):
        rows = slice(r * PROJ_ROWS, (r + 1) * PROJ_ROWS)
        xr = x_ref[0, rows, :]
        if dil == 1:
            tabs = (cos_ref[0, rows, :], sp_ref[0, rows, :], sm_ref[0, rows, :])
        else:
            xr = jnp.dot(perm_ref[...], xr, preferred_element_type=F32).astype(BF16)
            tabs = tuple(
                jnp.concatenate([tb.at[0][pl.ds(r * PROJ_ROWS + ph, sub, stride=dil), :] for ph in range(dil)], axis=0)
                for tb in (cos_ref, sp_ref, sm_ref))
        for kind in range(3):
            cols = slice(kind * D_MODEL, (kind + 1) * D_MODEL)
            acc = jnp.dot(xr, w_ref[:, cols], preferred_element_type=F32) + b_ref[:, cols]
            cs, sp, sm = [tb * Q_SCALE for tb in tabs] if kind == 0 else tabs
            for c in range(N_CHUNKS):
                sl = slice(c * LANES, (c + 1) * LANES)
                osl = slice(kind * D_MODEL + c * LANES, kind * D_MODEL + (c + 1) * LANES)
                y = acc[:, sl]
                if kind < 2:
                    y = y * cs + pltpu.roll(y, ROT_DIM // 2, 1) * sp + pltpu.roll(y, LANES - ROT_DIM // 2, 1) * sm
                if dil > 1:
                    yb = y.astype(BF16)
                    for ph in range(dil):
                        qk_ref[0, ph, r * sub:(r + 1) * sub, osl] = yb[ph * sub:(ph + 1) * sub]
                elif kind < 2:
                    qk_ref[0, 0, rows, osl] = y.astype(BF16)
                else:
                    for tl in range(PROJ_ROWS // TILE):
                        vt_ref[0, r * (PROJ_ROWS // TILE) + tl, sl, :] = y[tl * TILE:(tl + 1) * TILE].T.astype(BF16)
                if with_kmean and kind == 1:
                    km_ref[0, r, :, sl] = jnp.sum(y, axis=0, keepdims=True) * (1.0 / MOBA_BLOCK)


def _qkv_proj(xb, w, b, tables, *, dil=1, with_kmean=False, tm=512):
    bsz, seq, _ = xb.shape
    tm = min(tm, seq)
    assert PROJ_ROWS == MOBA_BLOCK and tm % PROJ_ROWS == 0 and PROJ_ROWS % (16 * dil) == 0
    tabs = [tb.reshape(bsz, seq, LANES) for tb in tables]
    tab_spec = pl.BlockSpec((1, tm, LANES), lambda bi, i: (bi, i, 0))
    fix = lambda bi, i: (0, 0)
    once = dict(pipeline_mode=pl.Buffered(1))
    ncol = 2 * D_MODEL if dil == 1 else 3 * D_MODEL
    out_specs = [pl.BlockSpec((1, dil, tm // dil, ncol), lambda bi, i: (bi, 0, i, 0))]
    out_shape = [jax.ShapeDtypeStruct((bsz, dil, seq // dil, ncol), BF16)]
    if dil == 1:
        out_specs.append(pl.BlockSpec((1, tm // TILE, D_MODEL, TILE), lambda bi, i: (bi, i, 0, 0)))
        out_shape.append(jax.ShapeDtypeStruct((bsz, seq // TILE, D_MODEL, TILE), BF16))
    if with_kmean:
        out_specs.append(pl.BlockSpec((1, tm // MOBA_BLOCK, 1, D_MODEL), lambda bi, i: (bi, i, 0, 0)))
        out_shape.append(jax.ShapeDtypeStruct((bsz, seq // MOBA_BLOCK, 1, D_MODEL), F32))
    in_specs = [
        pl.BlockSpec((1, tm, D_MODEL), lambda bi, i: (bi, i, 0)),
        pl.BlockSpec((D_MODEL, 3 * D_MODEL), fix, **once),
        pl.BlockSpec((1, 3 * D_MODEL), fix),
        tab_spec, tab_spec, tab_spec,
    ]
    args = [xb, w, b, *tabs]
    if dil > 1:
        p = np.arange(PROJ_ROWS)
        perm = np.zeros((PROJ_ROWS, PROJ_ROWS), np.float32)
        perm[p, (p % (PROJ_ROWS // dil)) * dil + p // (PROJ_ROWS // dil)] = 1.0
        in_specs.append(pl.BlockSpec((PROJ_ROWS, PROJ_ROWS), fix))
        args.append(jnp.asarray(perm, BF16))
    return pl.pallas_call(
        functools.partial(_proj_kernel, tm=tm, dil=dil, with_kmean=with_kmean),
        grid=(bsz, seq // tm),
        in_specs=in_specs,
        out_specs=out_specs,
        out_shape=out_shape,
        compiler_params=_params("parallel", "parallel"),
        name="qkv_proj",
    )(*args)


def _band_bias(n_back):
    k = np.arange(TILE)[:, None]
    q = (np.arange(2 * TILE) % TILE)[None, :]
    prev = q + TILE - k <= n_back
    own = k <= q
    return jnp.asarray(np.where(np.stack([prev, np.zeros_like(prev), own]), 0.0, NEG), F32)


def _stack_heads(q):
    lane = lax.broadcasted_iota(jnp.int32, q.shape, 1)
    zero = jnp.zeros_like(q)
    return jnp.concatenate([jnp.where(lane < HEAD_DIM, q, zero), jnp.where(lane >= HEAD_DIM, q, zero)], axis=0)


BAND_QBLOCKS = 4


def _band_attn_kernel(*refs, with_sink, with_lse, v_transposed, nq):
    refs = list(refs)
    sink_ref = refs.pop(0) if with_sink else None
    q_ref, kp_ref, kc_ref, vp_ref, vc_ref, bias_ref, o_ref = refs[:7]
    lse_ref = refs[7] if with_lse else None
    first = (pl.program_id(1) == 0).astype(jnp.int32)
    col = lax.broadcasted_iota(jnp.int32, (1, 2 * TILE), 1)
    for t in range(nq):
        rows = slice(t * TILE, (t + 1) * TILE)
        prow = slice((t - 1) * TILE, t * TILE)
        lse_rows = []
        for hp in range(HEAD_PAIRS):
            sl = slice(hp * LANES, (hp + 1) * LANES)
            qs = _stack_heads(q_ref[0, rows, sl])
            if t == 0:
                k_prev, bias_prev = kp_ref[0, :, sl], bias_ref[first]
                v_prev = vp_ref[0, 0, sl, :] if v_transposed else vp_ref[0, :, sl]
            else:
                k_prev, bias_prev = kc_ref[0, prow, sl], bias_ref[0]
                v_prev = vc_ref[0, t - 1, sl, :] if v_transposed else vc_ref[0, prow, sl]
            v_own = vc_ref[0, t, sl, :] if v_transposed else vc_ref[0, rows, sl]
            parts = []
            for k, v, bias in ((k_prev, v_prev, bias_prev), (kc_ref[0, rows, sl], v_own, bias_ref[2])):
                s = lax.dot_general(k, qs, _NT, preferred_element_type=F32) + bias
                m = jnp.max(s, axis=0, keepdims=True)
                p = jnp.exp2(s - m)
                l = jnp.sum(p, axis=0, keepdims=True)
                if v_transposed:
                    pv = jnp.dot(v, p.astype(BF16), preferred_element_type=F32)
                else:
                    pv = lax.dot_general(v, p.astype(BF16), _TN, preferred_element_type=F32)
                parts.append((m, l, pv))
            (m0, l0, pv0), (m1, l1, pv1) = parts
            m = jnp.maximum(m0, m1)
            if with_sink:
                sk = jnp.where(col < TILE, sink_ref[2 * hp], sink_ref[2 * hp + 1]) * LOG2E
                m = jnp.maximum(m, sk)
            a0, a1 = jnp.exp2(m0 - m), jnp.exp2(m1 - m)
            l = a0 * l0 + a1 * l1
            if with_sink:
                l = l + jnp.exp2(sk - m)
            inv = 1.0 / l
            w0, w1 = a0 * inv, a1 * inv
            o_t = jnp.concatenate(
                [pv0[:HEAD_DIM, :TILE] * w0[:, :TILE] + pv1[:HEAD_DIM, :TILE] * w1[:, :TILE],
                 pv0[HEAD_DIM:, TILE:] * w0[:, TILE:] + pv1[HEAD_DIM:, TILE:] * w1[:, TILE:]], axis=0)
            o_ref[0, rows, sl] = o_t.T.astype(BF16)
            if with_lse:
                lse = (m + jnp.log2(l)) * LN2
                lse_rows += [lse[:, :TILE], lse[:, TILE:]]
        if with_lse:
            pad = jnp.zeros((LANES - N_HEADS, TILE), F32)
            lse_ref[0, rows, :] = jnp.concatenate(lse_rows + [pad], axis=0).T


def _band_attention(qk, vt, *, n_back, sinks=None, with_lse=False):
    nseq, ln, _ = qk.shape
    nq = math.gcd(BAND_QBLOCKS, ln // TILE)
    blk = (1, nq * TILE, D_MODEL)
    cur = lambda c: pl.BlockSpec(blk, lambda s, i: (s, i, c))
    prev = lambda c: pl.BlockSpec((1, TILE, D_MODEL), lambda s, i: (s, jnp.maximum(i * nq - 1, 0), c))
    if vt is None:
        v_specs, v_args = [prev(2), cur(2)], [qk, qk]
    else:
        v_specs = [pl.BlockSpec((1, 1, D_MODEL, TILE), lambda s, i: (s, jnp.maximum(i * nq - 1, 0), 0, 0)),
                   pl.BlockSpec((1, nq, D_MODEL, TILE), lambda s, i: (s, i, 0, 0))]
        v_args = [vt, vt]
    in_specs = [cur(0), prev(1), cur(1)] + v_specs + [pl.BlockSpec((3, TILE, 2 * TILE), lambda s, i: (0, 0, 0))]
    args = [qk] * 3 + v_args + [_band_bias(n_back)]
    if sinks is not None:
        in_specs.insert(0, pl.BlockSpec(memory_space=pltpu.SMEM))
        args.insert(0, sinks)
    out_specs = [pl.BlockSpec(blk, lambda s, i: (s, i, 0))]
    out_shape = [jax.ShapeDtypeStruct((nseq, ln, D_MODEL), BF16)]
    if with_lse:
        out_specs.append(pl.BlockSpec((1, nq * TILE, LANES), lambda s, i: (s, i, 0)))
        out_shape.append(jax.ShapeDtypeStruct((nseq, ln, LANES), F32))
    return pl.pallas_call(
        functools.partial(_band_attn_kernel, with_sink=sinks is not None, with_lse=with_lse,
                          v_transposed=vt is not None, nq=nq),
        grid=(nseq, ln // (nq * TILE)),
        in_specs=in_specs,
        out_specs=out_specs,
        out_shape=out_shape,
        compiler_params=_params("parallel", "arbitrary"),
        name="band_attn",
    )(*args)


N_QSUB = MOBA_BLOCK // TILE
N_CHAINS = HEAD_PAIRS * N_QSUB
PAST_BLOCKS = 2


def _moba_kernel(q_ref, k_ref, vt_ref, km_ref, o_ref, qs_ref, sel_ref, m_ref, l_ref, acc_ref, *, nblk):
    qb = pl.program_id(1)
    blk = lax.broadcasted_iota(jnp.int32, (nblk, MOBA_BLOCK), 0)
    blkf = blk.astype(F32)
    lane = lax.broadcasted_iota(jnp.int32, (MOBA_BLOCK, LANES), 1)
    kidx = lax.broadcasted_iota(jnp.int32, (TILE, 2 * TILE), 0)
    qidx = lax.broadcasted_iota(jnp.int32, (TILE, 2 * TILE), 1) & (TILE - 1)
    causal = jnp.where(kidx <= qidx, 0.0, NEG).astype(F32)

    for hp in range(HEAD_PAIRS):
        sl = slice(hp * LANES, (hp + 1) * LANES)
        q = q_ref[0, :, sl]
        zero = jnp.zeros_like(q)
        sels = []
        for h in range(2):
            qm = jnp.where((lane >= h * HEAD_DIM) & (lane < (h + 1) * HEAD_DIM), q, zero)
            gate = lax.dot_general(km_ref[0, :, sl], qm.astype(F32), _NT, preferred_element_type=F32,
                                   precision=lax.Precision.HIGHEST)
            g = jnp.where(blk < qb, gate, NEG)
            sel = jnp.zeros((nblk, MOBA_BLOCK), F32)
            for r in range(min(MOBA_TOPK, nblk)):
                mx = jnp.max(g, axis=0, keepdims=True)
                idx = jnp.min(jnp.where(g == mx, blkf, float(nblk)), axis=0, keepdims=True)
                hit = blkf == idx
                sel = jnp.where(hit & (r < qb), 1.0, sel)
                g = jnp.where(hit, -jnp.inf, g)
            sels.append(sel)
        for qs in range(N_QSUB):
            c = hp * N_QSUB + qs
            rows = slice(qs * TILE, (qs + 1) * TILE)
            qs_ref[c] = _stack_heads(q[rows])
            sel_ref[c] = jnp.concatenate([sels[0][:, rows], sels[1][:, rows]], axis=1)
    m_ref[...] = jnp.full(m_ref.shape, NEG, F32)
    l_ref[...] = jnp.zeros(l_ref.shape, F32)
    acc_ref[...] = jnp.zeros(acc_ref.shape, F32)

    def update(c, hp, jt, bias_row, bias_tile):
        sl = slice(hp * LANES, (hp + 1) * LANES)
        off = pl.multiple_of(jt * TILE, TILE)
        s = lax.dot_general(k_ref[0, pl.ds(off, TILE), sl], qs_ref[c], _NT, preferred_element_type=F32)
        if bias_tile is not None:
            s = s + bias_tile
        mj = jnp.max(s, axis=0, keepdims=True)
        m_old = m_ref[c]
        if bias_row is None:
            m_new = jnp.maximum(m_old, mj)
            p = jnp.exp2(s - m_new)
        else:
            m_new = jnp.maximum(m_old, jnp.where(bias_row, mj, NEG))
            p = jnp.exp2(s + jnp.where(bias_row, -m_new, NEG))
        alpha = jnp.exp2(m_old - m_new)
        m_ref[c] = m_new
        l_ref[c] = alpha * l_ref[c] + jnp.sum(p, axis=0, keepdims=True)
        pv = jnp.dot(vt_ref[0, jt, sl, :], p.astype(BF16), preferred_element_type=F32)
        acc_ref[c] = alpha * acc_ref[c] + jnp.concatenate([pv[:HEAD_DIM, :TILE], pv[HEAD_DIM:, TILE:]], axis=1)

    def past_blocks(j0, nb):
        for u in range(nb * N_QSUB):
            j = j0 + u // N_QSUB
            for c in range(N_CHAINS):
                update(c, c // N_QSUB, j0 * N_QSUB + u, sel_ref[c, pl.ds(j, 1), :] > 0.5, None)

    def past(jj, carry):
        past_blocks(jj * PAST_BLOCKS, PAST_BLOCKS)
        return carry

    lax.fori_loop(0, qb // PAST_BLOCKS, past, 0)
    for rem in range(1, PAST_BLOCKS):
        @pl.when(qb % PAST_BLOCKS >= rem)
        def _():
            past_blocks(qb - qb % PAST_BLOCKS + rem - 1, 1)

    for c in range(N_CHAINS):
        hp, qs = c // N_QSUB, c % N_QSUB
        for kt in range(qs + 1):
            update(c, hp, qb * N_QSUB + kt, None, causal if kt == qs else None)

    for c in range(N_CHAINS):
        hp, qs = c // N_QSUB, c % N_QSUB
        o = acc_ref[c] * (1.0 / l_ref[c])
        o_t = jnp.concatenate([o[:, :TILE], o[:, TILE:]], axis=0)
        o_ref[0, qs * TILE:(qs + 1) * TILE, hp * LANES:(hp + 1) * LANES] = o_t.T.astype(BF16)


def _moba_attention(qk, vt, kmean):
    bsz, seq, _ = qk.shape
    nblk = seq // MOBA_BLOCK
    whole = dict(pipeline_mode=pl.Buffered(1))
    return pl.pallas_call(
        functools.partial(_moba_kernel, nblk=nblk),
        grid=(bsz, nblk),
        in_specs=[
            pl.BlockSpec((1, MOBA_BLOCK, D_MODEL), lambda b, i: (b, i, 0)),
            pl.BlockSpec((1, seq, D_MODEL), lambda b, i: (b, 0, 1), **whole),
            pl.BlockSpec((1, seq // TILE, D_MODEL, TILE), lambda b, i: (b, 0, 0, 0), **whole),
            pl.BlockSpec((1, nblk, D_MODEL), lambda b, i: (b, 0, 0)),
        ],
        out_specs=pl.BlockSpec((1, MOBA_BLOCK, D_MODEL), lambda b, i: (b, i, 0)),
        out_shape=jax.ShapeDtypeStruct((bsz, seq, D_MODEL), BF16),
        scratch_shapes=[pltpu.VMEM((N_CHAINS, 2 * TILE, LANES), BF16),
                        pltpu.VMEM((N_CHAINS, nblk, 2 * TILE), F32),
                        pltpu.VMEM((N_CHAINS, 1, 2 * TILE), F32),
                        pltpu.VMEM((N_CHAINS, 1, 2 * TILE), F32),
                        pltpu.VMEM((N_CHAINS, HEAD_DIM, 2 * TILE), F32)],
        compiler_params=_params("parallel", "arbitrary"),
        name="moba_attn",
    )(qk, qk, vt, kmean)


def _layer_norm(y, g, b):
    mu = jnp.mean(y, axis=-1, keepdims=True)
    d = y - mu
    var = jnp.mean(d * d, axis=-1, keepdims=True)
    return d * lax.rsqrt(var + LN_EPS) * g + b


def _oproj_ln_kernel(*refs, dils, tm):
    n = len(dils)
    o_refs, refs = refs[:n], refs[n:]
    if n > 1:
        lse_refs, e_ref, refs = refs[:n], refs[n], refs[n + 1:]
    w_ref, b_ref, x_ref, g_ref, beta_ref, out_ref, outb_ref = refs[:7]
    scr = refs[7:]
    if n == 1:
        o = o_refs[0][0, 0]
    else:
        o_scr, lse_scr = scr
        lses = []
        for gi, d in enumerate(dils):
            if d == 1:
                lses.append(lse_refs[gi][0, 0])
            else:
                for ph in range(d):
                    lse_scr.at[gi][pl.ds(ph, tm // d, stride=d), :] = lse_refs[gi][0, ph]
                lses.append(lse_scr[gi])
        mx = functools.reduce(jnp.maximum, lses)
        es = [jnp.exp(l - mx) for l in lses]
        inv = 1.0 / functools.reduce(lambda a, c: a + c, es)
        chunks = [None] * N_CHUNKS
        for gi, d in enumerate(dils):
            wgt = es[gi] * inv
            hi = wgt.astype(BF16)
            lo = (wgt - hi.astype(F32)).astype(BF16)
            wide = (jnp.dot(hi, e_ref[...], preferred_element_type=F32)
                    + jnp.dot(lo, e_ref[...], preferred_element_type=F32))
            for c in range(N_CHUNKS):
                sl = slice(c * LANES, (c + 1) * LANES)
                if d == 1:
                    og = o_refs[gi][0, 0, :, sl].astype(F32)
                else:
                    for ph in range(d):
                        o_scr.at[gi, c][pl.ds(ph, tm // d, stride=d), :] = o_refs[gi][0, ph, :, sl].astype(F32)
                    og = o_scr[gi, c]
                term = wide[:, sl] * og
                chunks[c] = term if chunks[c] is None else chunks[c] + term
        o = jnp.concatenate(chunks, axis=1).astype(BF16)
    h = jnp.dot(o, w_ref[...], preferred_element_type=F32) + b_ref[...]
    y = _layer_norm(DN_ALPHA * x_ref[0] + h, g_ref[...], beta_ref[...])
    out_ref[0] = y
    outb_ref[0] = y.astype(BF16)


def _oproj_ln(os_, lses, dils, w, b, x, g, beta, *, tm=512):
    bsz, seq, _ = x.shape
    tm = min(tm, seq)
    n = len(os_)
    tok = lambda bi, i: (bi, i, 0)
    ph = lambda bi, i: (bi, 0, i, 0)
    fix = lambda bi, i: (0, 0)
    in_specs = [pl.BlockSpec((1, d, tm // d, D_MODEL), ph) for d in dils]
    args = list(os_)
    scratch = []
    if n > 1:
        in_specs += [pl.BlockSpec((1, d, tm // d, LANES), ph) for d in dils]
        args += list(lses)
        expand = np.zeros((LANES, D_MODEL), np.float32)
        for hd in range(N_HEADS):
            expand[hd, hd * HEAD_DIM:(hd + 1) * HEAD_DIM] = 1.0
        in_specs.append(pl.BlockSpec((LANES, D_MODEL), fix))
        args.append(jnp.asarray(expand, BF16))
        scratch = [pltpu.VMEM((n, N_CHUNKS, tm, LANES), F32), pltpu.VMEM((n, tm, LANES), F32)]
    in_specs += [pl.BlockSpec((D_MODEL, D_MODEL), fix), pl.BlockSpec((1, D_MODEL), fix),
                 pl.BlockSpec((1, tm, D_MODEL), tok), pl.BlockSpec((1, D_MODEL), fix), pl.BlockSpec((1, D_MODEL), fix)]
    args += [w, b, x, g, beta]
    return pl.pallas_call(
        functools.partial(_oproj_ln_kernel, dils=tuple(dils), tm=tm),
        grid=(bsz, seq // tm),
        in_specs=in_specs,
        out_specs=[pl.BlockSpec((1, tm, D_MODEL), tok), pl.BlockSpec((1, tm, D_MODEL), tok)],
        out_shape=[jax.ShapeDtypeStruct((bsz, seq, D_MODEL), F32), jax.ShapeDtypeStruct((bsz, seq, D_MODEL), BF16)],
        scratch_shapes=scratch,
        compiler_params=_params("parallel", "parallel"),
        name="oproj_ln",
    )(*args)


def _ffn_ln_kernel(x_ref, xb_ref, wg_ref, wu_ref, wd_ref, g_ref, beta_ref, out_ref, outb_ref):
    xb = xb_ref[...]
    gate = jnp.dot(xb, wg_ref[...], preferred_element_type=F32)
    up = jnp.dot(xb, wu_ref[...], preferred_element_type=F32)
    hid = (gate * jax.nn.sigmoid(gate) * up).astype(BF16)
    y = jnp.dot(hid, wd_ref[...], preferred_element_type=F32)
    y = _layer_norm(DN_ALPHA * x_ref[...] + y, g_ref[...], beta_ref[...])
    out_ref[...] = y
    outb_ref[...] = y.astype(BF16)


def _ffn_ln(x, xb, wg, wu, wd, g, beta, *, tm=512):
    t = x.shape[0]
    tm = min(tm, t)
    row = lambda i: (i, 0)
    fix = lambda i: (0, 0)
    resident = dict(pipeline_mode=pl.Buffered(1))
    return pl.pallas_call(
        _ffn_ln_kernel,
        grid=(t // tm,),
        in_specs=[pl.BlockSpec((tm, D_MODEL), row), pl.BlockSpec((tm, D_MODEL), row),
                  pl.BlockSpec((D_MODEL, D_FF), fix, **resident),
                  pl.BlockSpec((D_MODEL, D_FF), fix, **resident),
                  pl.BlockSpec((D_FF, D_MODEL), fix, **resident),
                  pl.BlockSpec((1, D_MODEL), fix), pl.BlockSpec((1, D_MODEL), fix)],
        out_specs=[pl.BlockSpec((tm, D_MODEL), row), pl.BlockSpec((tm, D_MODEL), row)],
        out_shape=[jax.ShapeDtypeStruct((t, D_MODEL), F32), jax.ShapeDtypeStruct((t, D_MODEL), BF16)],
        compiler_params=_params("parallel"),
        name="ffn_ln",
    )(x, xb, wg, wu, wd, g, beta)


def _expand_gqa_columns():
    nq = N_HEADS * HEAD_DIM
    nkv = A_KV_HEADS * HEAD_DIM
    grp = N_HEADS // A_KV_HEADS
    head = np.arange(nq) // HEAD_DIM
    kv_col = (head // grp) * HEAD_DIM + np.arange(nq) % HEAD_DIM
    return np.concatenate([np.arange(nq), nq + kv_col, nq + nkv + kv_col])


def kernel(x, positions, ln_g, ln_b, a_w_qkv, a_b_qkv, a_sinks, a_w_o, a_b_o, b_w_qkv, b_w_o, c_w_qkv, c_w_o,
           w_gate_up, w_down):
    bsz, seq, d = x.shape
    assert d == D_MODEL and seq % (B_GROUPS[-1][1] * TILE) == 0 and seq % MOBA_BLOCK == 0
    t = bsz * seq
    tables = _rope_tables(positions.reshape(t, 1))
    xf, xb = x, x.astype(BF16)
    gqa_cols = _expand_gqa_columns()
    zero_bias = jnp.zeros((1, D_MODEL), F32)
    zero_bias3 = jnp.zeros((1, 3 * D_MODEL), F32)
    row = lambda v: v.reshape(1, -1)
    for i in range(DEPTH):
        kind, j = i % 3, i // 3
        g0, b0, g1, b1 = row(ln_g[i, 0]), row(ln_b[i, 0]), row(ln_g[i, 1]), row(ln_b[i, 1])
        if kind == 0:
            w = a_w_qkv[j][:, gqa_cols].astype(BF16)
            qk, vt = _qkv_proj(xb, w, row(a_b_qkv[j][gqa_cols]), tables)
            o = _band_attention(qk[:, 0], vt, n_back=A_WINDOW - 1, sinks=a_sinks[j])[0]
            xf, xb = _oproj_ln([o[:, None]], None, (1,), a_w_o[j].astype(BF16), row(a_b_o[j]), xf, g0, b0)
        elif kind == 1:
            w = b_w_qkv[j].astype(BF16)
            os_, lses, dils = [], [], []
            for gi, (window, dil) in enumerate(B_GROUPS):
                cols = slice(3 * gi * D_MODEL, 3 * (gi + 1) * D_MODEL)
                ln = seq // dil
                if dil == 1:
                    qk, vt = _qkv_proj(xb, w[:, cols], zero_bias3, tables)
                    qk = qk[:, 0]
                else:
                    qk, vt = _qkv_proj(xb, w[:, cols], zero_bias3, tables, dil=dil)[0], None
                    qk = qk.reshape(bsz * dil, ln, 3 * D_MODEL)
                o, lse = _band_attention(qk, vt, n_back=window // dil, with_lse=True)
                os_.append(o.reshape(bsz, dil, ln, D_MODEL))
                lses.append(lse.reshape(bsz, dil, ln, LANES))
                dils.append(dil)
            xf, xb = _oproj_ln(os_, lses, dils, b_w_o[j].astype(BF16), zero_bias, xf, g0, b0)
        else:
            w = c_w_qkv[j].astype(BF16)
            qk, vt, kmean = _qkv_proj(xb, w, zero_bias3, tables, with_kmean=True)
            o = _moba_attention(qk[:, 0], vt, kmean[:, :, 0])
            xf, xb = _oproj_ln([o[:, None]], None, (1,), c_w_o[j].astype(BF16), zero_bias, xf, g0, b0)
        wgu = w_gate_up[i]
        y, yb = _ffn_ln(xf.reshape(t, d), xb.reshape(t, d), wgu[:, :D_FF].astype(BF16), wgu[:, D_FF:].astype(BF16),
                        w_down[i].astype(BF16), g1, b1)
        xf, xb = y.reshape(bsz, seq, d), yb.reshape(bsz, seq, d)
    return xf
```

```python
import functools
import math

import numpy as np
import jax
import jax.numpy as jnp
from jax import lax
from jax.experimental import pallas as pl
from jax.experimental.pallas import tpu as pltpu

D_MODEL = 1024
N_HEADS = 16
HEAD_DIM = 64
ROT_DIM = HEAD_DIM // 4
ROPE_THETA = 500000.0
A_KV_HEADS = 4
A_WINDOW = 128
B_GROUPS = ((128, 1), (512, 4), (2048, 16))
MOBA_BLOCK = 256
MOBA_TOPK = 3
D_FF = 2816
DEPTH = 4
DN_ALPHA = (2 * DEPTH) ** 0.25
LN_EPS = 1e-5
NEG = -1e30

LANES = 128
TILE = 128
HEAD_PAIRS = N_HEADS // 2
N_CHUNKS = D_MODEL // LANES
VMEM_LIMIT = 56 * 1024 * 1024
LOG2E = math.log2(math.e)
LN2 = math.log(2.0)
Q_SCALE = HEAD_DIM ** -0.5 * LOG2E

F32 = jnp.float32
BF16 = jnp.bfloat16
_NT = (((1,), (1,)), ((), ()))
_TN = (((0,), (0,)), ((), ()))


def _params(*sem):
    return pltpu.CompilerParams(dimension_semantics=sem, vmem_limit_bytes=VMEM_LIMIT)


_INV_FREQ = [float(np.float32(ROPE_THETA ** (-(2.0 * i) / ROT_DIM))) for i in range(ROT_DIM // 2)]


def _rope_table_kernel(pos_ref, cos_ref, sp_ref, sm_ref):
    pos = pos_ref[...].astype(F32)
    lane = lax.broadcasted_iota(jnp.int32, (1, LANES), 1)
    c = lane & (HEAD_DIM - 1)
    f = c & (ROT_DIM // 2 - 1)
    inv = jnp.zeros((1, LANES), F32)
    for i, v in enumerate(_INV_FREQ):
        inv = jnp.where(f == i, v, inv)
    ang = pos * inv
    cs, sn = jnp.cos(ang), jnp.sin(ang)
    cos_ref[...] = jnp.where(c < ROT_DIM, cs, 1.0)
    sp_ref[...] = jnp.where((c >= ROT_DIM // 2) & (c < ROT_DIM), sn, 0.0)
    sm_ref[...] = jnp.where(c < ROT_DIM // 2, -sn, 0.0)


def _rope_tables(pos_col):
    t = pos_col.shape[0]
    tm = min(t, 2048)
    spec = pl.BlockSpec((tm, LANES), lambda i: (i, 0))
    shp = jax.ShapeDtypeStruct((t, LANES), F32)
    return pl.pallas_call(
        _rope_table_kernel,
        grid=(t // tm,),
        in_specs=[pl.BlockSpec((tm, 1), lambda i: (i, 0))],
        out_specs=[spec, spec, spec],
        out_shape=[shp, shp, shp],
        compiler_params=_params("parallel"),
        name="rope_tables",
    )(pos_col)


PROJ_ROWS = 256


def _rotary(y, cs, sp, sm):
    return y * cs + pltpu.roll(y, ROT_DIM // 2, 1) * sp + pltpu.roll(y, LANES - ROT_DIM // 2, 1) * sm


def _proj_shared_kv(xr, w_ref, b_ref, tabs, qk_ref, vt_ref, kind, r):
    kv = A_KV_HEADS * HEAD_DIM
    base = D_MODEL + (kind - 1) * kv
    acc = jnp.dot(xr, w_ref[:, base:base + kv], preferred_element_type=F32) + b_ref[:, base:base + kv]
    lane = lax.broadcasted_iota(jnp.int32, (PROJ_ROWS, LANES), 1)
    rows = slice(r * PROJ_ROWS, (r + 1) * PROJ_ROWS)
    pairs_per_kv = N_HEADS // A_KV_HEADS // 2
    for cc in range(kv // LANES):
        y = acc[:, cc * LANES:(cc + 1) * LANES]
        if kind == 1:
            y = _rotary(y, *tabs)
        swapped = pltpu.roll(y, HEAD_DIM, 1)
        for par, both in enumerate((jnp.where(lane < HEAD_DIM, y, swapped), jnp.where(lane < HEAD_DIM, swapped, y))):
            g = 2 * cc + par
            if kind == 2:
                tiles = [both[tl * TILE:(tl + 1) * TILE].T.astype(BF16) for tl in range(PROJ_ROWS // TILE)]
            for hp in range(g * pairs_per_kv, (g + 1) * pairs_per_kv):
                sl = slice(hp * LANES, (hp + 1) * LANES)
                if kind == 1:
                    qk_ref[0, 0, rows, D_MODEL + hp * LANES:D_MODEL + (hp + 1) * LANES] = both.astype(BF16)
                else:
                    for tl, tile in enumerate(tiles):
                        vt_ref[0, r * (PROJ_ROWS // TILE) + tl, sl, :] = tile


def _proj_kernel(x_ref, w_ref, b_ref, cos_ref, sp_ref, sm_ref, *rest, tm, dil, with_kmean, gqa):
    rest = list(rest)
    perm_ref = rest.pop(0) if dil > 1 else None
    qk_ref = rest.pop(0)
    vt_ref = rest.pop(0) if dil == 1 else None
    km_ref = rest.pop(0) if with_kmean else None
    sub = PROJ_ROWS // dil
    for r in range(tm // PROJ_ROWS):
        rows = slice(r * PROJ_ROWS, (r + 1) * PROJ_ROWS)
        xr = x_ref[0, rows, :]
        if dil == 1:
            tabs = (cos_ref[0, rows, :], sp_ref[0, rows, :], sm_ref[0, rows, :])
        else:
            xr = jnp.dot(perm_ref[...], xr, preferred_element_type=F32).astype(BF16)
            tabs = tuple(
                jnp.concatenate([tb.at[0][pl.ds(r * PROJ_ROWS + ph, sub, stride=dil), :] for ph in range(dil)], axis=0)
                for tb in (cos_ref, sp_ref, sm_ref))
        for kind in range(3):
            if gqa and kind > 0:
                _proj_shared_kv(xr, w_ref, b_ref, tabs, qk_ref, vt_ref, kind, r)
                continue
            cols = slice(kind * D_MODEL, (kind + 1) * D_MODEL)
            acc = jnp.dot(xr, w_ref[:, cols], preferred_element_type=F32) + b_ref[:, cols]
            cs, sp, sm = [tb * Q_SCALE for tb in tabs] if kind == 0 else tabs
            for c in range(N_CHUNKS):
                sl = slice(c * LANES, (c + 1) * LANES)
                osl = slice(kind * D_MODEL + c * LANES, kind * D_MODEL + (c + 1) * LANES)
                y = acc[:, sl]
                if kind < 2:
                    y = _rotary(y, cs, sp, sm)
                if dil > 1:
                    yb = y.astype(BF16)
                    for ph in range(dil):
                        qk_ref[0, ph, r * sub:(r + 1) * sub, osl] = yb[ph * sub:(ph + 1) * sub]
                elif kind < 2:
                    qk_ref[0, 0, rows, osl] = y.astype(BF16)
                else:
                    for tl in range(PROJ_ROWS // TILE):
                        vt_ref[0, r * (PROJ_ROWS // TILE) + tl, sl, :] = y[tl * TILE:(tl + 1) * TILE].T.astype(BF16)
                if with_kmean and kind == 1:
                    km_ref[0, r, :, sl] = jnp.sum(y, axis=0, keepdims=True) * (1.0 / MOBA_BLOCK)


def _qkv_proj(xb, w, b, tables, *, dil=1, with_kmean=False, tm=512):
    bsz, seq, _ = xb.shape
    tm = min(tm, seq)
    gqa = w.shape[1] == D_MODEL + 2 * A_KV_HEADS * HEAD_DIM
    assert gqa or w.shape[1] == 3 * D_MODEL
    assert not gqa or (dil == 1 and not with_kmean)
    assert PROJ_ROWS == MOBA_BLOCK and tm % PROJ_ROWS == 0 and PROJ_ROWS % (16 * dil) == 0
    tabs = [tb.reshape(bsz, seq, LANES) for tb in tables]
    tab_spec = pl.BlockSpec((1, tm, LANES), lambda bi, i: (bi, i, 0))
    fix = lambda bi, i: (0, 0)
    once = dict(pipeline_mode=pl.Buffered(1))
    ncol = 2 * D_MODEL if dil == 1 else 3 * D_MODEL
    out_specs = [pl.BlockSpec((1, dil, tm // dil, ncol), lambda bi, i: (bi, 0, i, 0))]
    out_shape = [jax.ShapeDtypeStruct((bsz, dil, seq // dil, ncol), BF16)]
    if dil == 1:
        out_specs.append(pl.BlockSpec((1, tm // TILE, D_MODEL, TILE), lambda bi, i: (bi, i, 0, 0)))
        out_shape.append(jax.ShapeDtypeStruct((bsz, seq // TILE, D_MODEL, TILE), BF16))
    if with_kmean:
        out_specs.append(pl.BlockSpec((1, tm // MOBA_BLOCK, 1, D_MODEL), lambda bi, i: (bi, i, 0, 0)))
        out_shape.append(jax.ShapeDtypeStruct((bsz, seq // MOBA_BLOCK, 1, D_MODEL), F32))
    in_specs = [
        pl.BlockSpec((1, tm, D_MODEL), lambda bi, i: (bi, i, 0)),
        pl.BlockSpec((D_MODEL, w.shape[1]), fix, **once),
        pl.BlockSpec((1, w.shape[1]), fix),
        tab_spec, tab_spec, tab_spec,
    ]
    args = [xb, w, b, *tabs]
    if dil > 1:
        p = np.arange(PROJ_ROWS)
        perm = np.zeros((PROJ_ROWS, PROJ_ROWS), np.float32)
        perm[p, (p % (PROJ_ROWS // dil)) * dil + p // (PROJ_ROWS // dil)] = 1.0
        in_specs.append(pl.BlockSpec((PROJ_ROWS, PROJ_ROWS), fix))
        args.append(jnp.asarray(perm, BF16))
    return pl.pallas_call(
        functools.partial(_proj_kernel, tm=tm, dil=dil, with_kmean=with_kmean, gqa=gqa),
        grid=(bsz, seq // tm),
        in_specs=in_specs,
        out_specs=out_specs,
        out_shape=out_shape,
        compiler_params=_params("parallel", "parallel"),
        name="qkv_proj",
    )(*args)


def _band_bias(n_back):
    k = np.arange(TILE)[:, None]
    q = (np.arange(2 * TILE) % TILE)[None, :]
    prev = q + TILE - k <= n_back
    own = k <= q
    return jnp.asarray(np.where(np.stack([prev, np.zeros_like(prev), own]), 0.0, NEG), F32)


def _stack_heads(q):
    lane = lax.broadcasted_iota(jnp.int32, q.shape, 1)
    zero = jnp.zeros_like(q)
    return jnp.concatenate([jnp.where(lane < HEAD_DIM, q, zero), jnp.where(lane >= HEAD_DIM, q, zero)], axis=0)


BAND_QBLOCKS = 4


def _band_attn_kernel(*refs, with_sink, with_lse, v_transposed, nq):
    refs = list(refs)
    sink_ref = refs.pop(0) if with_sink else None
    q_ref, kp_ref, kc_ref, vp_ref, vc_ref, bias_ref, o_ref = refs[:7]
    lse_ref = refs[7] if with_lse else None
    first = (pl.program_id(1) == 0).astype(jnp.int32)
    col = lax.broadcasted_iota(jnp.int32, (1, 2 * TILE), 1)
    for t in range(nq):
        rows = slice(t * TILE, (t + 1) * TILE)
        prow = slice((t - 1) * TILE, t * TILE)
        lse_rows = []
        for hp in range(HEAD_PAIRS):
            sl = slice(hp * LANES, (hp + 1) * LANES)
            qs = _stack_heads(q_ref[0, rows, sl])
            if t == 0:
                k_prev, bias_prev = kp_ref[0, :, sl], bias_ref[first]
                v_prev = vp_ref[0, 0, sl, :] if v_transposed else vp_ref[0, :, sl]
            else:
                k_prev, bias_prev = kc_ref[0, prow, sl], bias_ref[0]
                v_prev = vc_ref[0, t - 1, sl, :] if v_transposed else vc_ref[0, prow, sl]
            v_own = vc_ref[0, t, sl, :] if v_transposed else vc_ref[0, rows, sl]
            parts = []
            for k, v, bias in ((k_prev, v_prev, bias_prev), (kc_ref[0, rows, sl], v_own, bias_ref[2])):
                s = lax.dot_general(k, qs, _NT, preferred_element_type=F32) + bias
                m = jnp.max(s, axis=0, keepdims=True)
                p = jnp.exp2(s - m)
                l = jnp.sum(p, axis=0, keepdims=True)
                if v_transposed:
                    pv = jnp.dot(v, p.astype(BF16), preferred_element_type=F32)
                else:
                    pv = lax.dot_general(v, p.astype(BF16), _TN, preferred_element_type=F32)
                parts.append((m, l, pv))
            (m0, l0, pv0), (m1, l1, pv1) = parts
            m = jnp.maximum(m0, m1)
            if with_sink:
                sk = jnp.where(col < TILE, sink_ref[2 * hp], sink_ref[2 * hp + 1]) * LOG2E
                m = jnp.maximum(m, sk)
            a0, a1 = jnp.exp2(m0 - m), jnp.exp2(m1 - m)
            l = a0 * l0 + a1 * l1
            if with_sink:
                l = l + jnp.exp2(sk - m)
            inv = 1.0 / l
            w0, w1 = a0 * inv, a1 * inv
            o_t = jnp.concatenate(
                [pv0[:HEAD_DIM, :TILE] * w0[:, :TILE] + pv1[:HEAD_DIM, :TILE] * w1[:, :TILE],
                 pv0[HEAD_DIM:, TILE:] * w0[:, TILE:] + pv1[HEAD_DIM:, TILE:] * w1[:, TILE:]], axis=0)
            o_ref[0, rows, sl] = o_t.T.astype(BF16)
            if with_lse:
                lse = (m + jnp.log2(l)) * LN2
                lse_rows += [lse[:, :TILE], lse[:, TILE:]]
        if with_lse:
            pad = jnp.zeros((LANES - N_HEADS, TILE), F32)
            lse_ref[0, rows, :] = jnp.concatenate(lse_rows + [pad], axis=0).T


def _band_attention(qk, vt, *, n_back, sinks=None, with_lse=False):
    nseq, ln, _ = qk.shape
    nq = math.gcd(BAND_QBLOCKS, ln // TILE)
    blk = (1, nq * TILE, D_MODEL)
    cur = lambda c: pl.BlockSpec(blk, lambda s, i: (s, i, c))
    prev = lambda c: pl.BlockSpec((1, TILE, D_MODEL), lambda s, i: (s, jnp.maximum(i * nq - 1, 0), c))
    if vt is None:
        v_specs, v_args = [prev(2), cur(2)], [qk, qk]
    else:
        v_specs = [pl.BlockSpec((1, 1, D_MODEL, TILE), lambda s, i: (s, jnp.maximum(i * nq - 1, 0), 0, 0)),
                   pl.BlockSpec((1, nq, D_MODEL, TILE), lambda s, i: (s, i, 0, 0))]
        v_args = [vt, vt]
    in_specs = [cur(0), prev(1), cur(1)] + v_specs + [pl.BlockSpec((3, TILE, 2 * TILE), lambda s, i: (0, 0, 0))]
    args = [qk] * 3 + v_args + [_band_bias(n_back)]
    if sinks is not None:
        in_specs.insert(0, pl.BlockSpec(memory_space=pltpu.SMEM))
        args.insert(0, sinks)
    out_specs = [pl.BlockSpec(blk, lambda s, i: (s, i, 0))]
    out_shape = [jax.ShapeDtypeStruct((nseq, ln, D_MODEL), BF16)]
    if with_lse:
        out_specs.append(pl.BlockSpec((1, nq * TILE, LANES), lambda s, i: (s, i, 0)))
        out_shape.append(jax.ShapeDtypeStruct((nseq, ln, LANES), F32))
    return pl.pallas_call(
        functools.partial(_band_attn_kernel, with_sink=sinks is not None, with_lse=with_lse,
                          v_transposed=vt is not None, nq=nq),
        grid=(nseq, ln // (nq * TILE)),
        in_specs=in_specs,
        out_specs=out_specs,
        out_shape=out_shape,
        compiler_params=_params("parallel", "arbitrary"),
        name="band_attn",
    )(*args)


N_QSUB = MOBA_BLOCK // TILE
N_CHAINS = HEAD_PAIRS * N_QSUB
PAST_BLOCKS = 2


def _moba_kernel(q_ref, k_ref, vt_ref, km_ref, o_ref, qs_ref, sel_ref, m_ref, l_ref, acc_ref, *, nblk):
    qb = pl.program_id(1)
    blk = lax.broadcasted_iota(jnp.int32, (nblk, MOBA_BLOCK), 0)
    blkf = blk.astype(F32)
    lane = lax.broadcasted_iota(jnp.int32, (MOBA_BLOCK, LANES), 1)
    kidx = lax.broadcasted_iota(jnp.int32, (TILE, 2 * TILE), 0)
    qidx = lax.broadcasted_iota(jnp.int32, (TILE, 2 * TILE), 1) & (TILE - 1)
    causal = jnp.where(kidx <= qidx, 0.0, NEG).astype(F32)

    for hp in range(HEAD_PAIRS):
        sl = slice(hp * LANES, (hp + 1) * LANES)
        q = q_ref[0, :, sl]
        zero = jnp.zeros_like(q)
        sels = []
        for h in range(2):
            qm = jnp.where((lane >= h * HEAD_DIM) & (lane < (h + 1) * HEAD_DIM), q, zero)
            gate = lax.dot_general(km_ref[0, :, sl], qm.astype(F32), _NT, preferred_element_type=F32,
                                   precision=lax.Precision.HIGHEST)
            g = jnp.where(blk < qb, gate, NEG)
            sel = jnp.zeros((nblk, MOBA_BLOCK), F32)
            for r in range(min(MOBA_TOPK, nblk)):
                mx = jnp.max(g, axis=0, keepdims=True)
                idx = jnp.min(jnp.where(g == mx, blkf, float(nblk)), axis=0, keepdims=True)
                hit = blkf == idx
                sel = jnp.where(hit & (r < qb), 1.0, sel)
                g = jnp.where(hit, -jnp.inf, g)
            sels.append(sel)
        for qs in range(N_QSUB):
            c = hp * N_QSUB + qs
            rows = slice(qs * TILE, (qs + 1) * TILE)
            qs_ref[c] = _stack_heads(q[rows])
            sel_ref[c] = jnp.concatenate([sels[0][:, rows], sels[1][:, rows]], axis=1)
    m_ref[...] = jnp.full(m_ref.shape, NEG, F32)
    l_ref[...] = jnp.zeros(l_ref.shape, F32)
    acc_ref[...] = jnp.zeros(acc_ref.shape, F32)

    def update(c, hp, jt, bias_row, bias_tile):
        sl = slice(hp * LANES, (hp + 1) * LANES)
        off = pl.multiple_of(jt * TILE, TILE)
        s = lax.dot_general(k_ref[0, pl.ds(off, TILE), sl], qs_ref[c], _NT, preferred_element_type=F32)
        if bias_tile is not None:
            s = s + bias_tile
        mj = jnp.max(s, axis=0, keepdims=True)
        m_old = m_ref[c]
        if bias_row is None:
            m_new = jnp.maximum(m_old, mj)
            p = jnp.exp2(s - m_new)
        else:
            m_new = jnp.maximum(m_old, jnp.where(bias_row, mj, NEG))
            p = jnp.exp2(s + jnp.where(bias_row, -m_new, NEG))
        alpha = jnp.exp2(m_old - m_new)
        m_ref[c] = m_new
        l_ref[c] = alpha * l_ref[c] + jnp.sum(p, axis=0, keepdims=True)
        pv = jnp.dot(vt_ref[0, jt, sl, :], p.astype(BF16), preferred_element_type=F32)
        acc_ref[c] = alpha * acc_ref[c] + jnp.concatenate([pv[:HEAD_DIM, :TILE], pv[HEAD_DIM:, TILE:]], axis=1)

    def past_blocks(j0, nb):
        for u in range(nb * N_QSUB):
            j = j0 + u // N_QSUB
            for c in range(N_CHAINS):
                update(c, c // N_QSUB, j0 * N_QSUB + u, sel_ref[c, pl.ds(j, 1), :] > 0.5, None)

    def past(jj, carry):
        past_blocks(jj * PAST_BLOCKS, PAST_BLOCKS)
        return carry

    lax.fori_loop(0, qb // PAST_BLOCKS, past, 0)
    for rem in range(1, PAST_BLOCKS):
        @pl.when(qb % PAST_BLOCKS >= rem)
        def _():
            past_blocks(qb - qb % PAST_BLOCKS + rem - 1, 1)

    for c in range(N_CHAINS):
        hp, qs = c // N_QSUB, c % N_QSUB
        for kt in range(qs + 1):
            update(c, hp, qb * N_QSUB + kt, None, causal if kt == qs else None)

    for c in range(N_CHAINS):
        hp, qs = c // N_QSUB, c % N_QSUB
        o = acc_ref[c] * (1.0 / l_ref[c])
        o_t = jnp.concatenate([o[:, :TILE], o[:, TILE:]], axis=0)
        o_ref[0, qs * TILE:(qs + 1) * TILE, hp * LANES:(hp + 1) * LANES] = o_t.T.astype(BF16)


def _moba_attention(qk, vt, kmean):
    bsz, seq, _ = qk.shape
    nblk = seq // MOBA_BLOCK
    whole = dict(pipeline_mode=pl.Buffered(1))
    return pl.pallas_call(
        functools.partial(_moba_kernel, nblk=nblk),
        grid=(bsz, nblk),
        in_specs=[
            pl.BlockSpec((1, MOBA_BLOCK, D_MODEL), lambda b, i: (b, i, 0)),
            pl.BlockSpec((1, seq, D_MODEL), lambda b, i: (b, 0, 1), **whole),
            pl.BlockSpec((1, seq // TILE, D_MODEL, TILE), lambda b, i: (b, 0, 0, 0), **whole),
            pl.BlockSpec((1, nblk, D_MODEL), lambda b, i: (b, 0, 0)),
        ],
        out_specs=pl.BlockSpec((1, MOBA_BLOCK, D_MODEL), lambda b, i: (b, i, 0)),
        out_shape=jax.ShapeDtypeStruct((bsz, seq, D_MODEL), BF16),
        scratch_shapes=[pltpu.VMEM((N_CHAINS, 2 * TILE, LANES), BF16),
                        pltpu.VMEM((N_CHAINS, nblk, 2 * TILE), F32),
                        pltpu.VMEM((N_CHAINS, 1, 2 * TILE), F32),
                        pltpu.VMEM((N_CHAINS, 1, 2 * TILE), F32),
                        pltpu.VMEM((N_CHAINS, HEAD_DIM, 2 * TILE), F32)],
        compiler_params=_params("parallel", "arbitrary"),
        name="moba_attn",
    )(qk, qk, vt, kmean)


def _layer_norm(y, g, b):
    mu = jnp.mean(y, axis=-1, keepdims=True)
    d = y - mu
    var = jnp.mean(d * d, axis=-1, keepdims=True)
    return d * lax.rsqrt(var + LN_EPS) * g + b


def _oproj_ln_kernel(*refs, dils, tm):
    n = len(dils)
    o_refs, refs = refs[:n], refs[n:]
    if n > 1:
        lse_refs, e_ref, refs = refs[:n], refs[n], refs[n + 1:]
    w_ref, b_ref, x_ref, g_ref, beta_ref, out_ref, outb_ref = refs[:7]
    scr = refs[7:]
    if n == 1:
        o = o_refs[0][0, 0]
    else:
        o_scr, lse_scr = scr
        lses = []
        for gi, d in enumerate(dils):
            if d == 1:
                lses.append(lse_refs[gi][0, 0])
            else:
                for ph in range(d):
                    lse_scr.at[gi][pl.ds(ph, tm // d, stride=d), :] = lse_refs[gi][0, ph]
                lses.append(lse_scr[gi])
        mx = functools.reduce(jnp.maximum, lses)
        es = [jnp.exp(l - mx) for l in lses]
        inv = 1.0 / functools.reduce(lambda a, c: a + c, es)
        chunks = [None] * N_CHUNKS
        for gi, d in enumerate(dils):
            wgt = es[gi] * inv
            hi = wgt.astype(BF16)
            lo = (wgt - hi.astype(F32)).astype(BF16)
            wide = (jnp.dot(hi, e_ref[...], preferred_element_type=F32)
                    + jnp.dot(lo, e_ref[...], preferred_element_type=F32))
            for c in range(N_CHUNKS):
                sl = slice(c * LANES, (c + 1) * LANES)
                if d == 1:
                    og = o_refs[gi][0, 0, :, sl].astype(F32)
                else:
                    for ph in range(d):
                        o_scr.at[gi, c][pl.ds(ph, tm // d, stride=d), :] = o_refs[gi][0, ph, :, sl].astype(F32)
                    og = o_scr[gi, c]
                term = wide[:, sl] * og
                chunks[c] = term if chunks[c] is None else chunks[c] + term
        o = jnp.concatenate(chunks, axis=1).astype(BF16)
    h = jnp.dot(o, w_ref[...], preferred_element_type=F32) + b_ref[...]
    y = _layer_norm(DN_ALPHA * x_ref[0] + h, g_ref[...], beta_ref[...])
    out_ref[0] = y
    outb_ref[0] = y.astype(BF16)


def _oproj_ln(os_, lses, dils, w, b, x, g, beta, *, tm=512):
    bsz, seq, _ = x.shape
    tm = min(tm, seq)
    n = len(os_)
    tok = lambda bi, i: (bi, i, 0)
    ph = lambda bi, i: (bi, 0, i, 0)
    fix = lambda bi, i: (0, 0)
    in_specs = [pl.BlockSpec((1, d, tm // d, D_MODEL), ph) for d in dils]
    args = list(os_)
    scratch = []
    if n > 1:
        in_specs += [pl.BlockSpec((1, d, tm // d, LANES), ph) for d in dils]
        args += list(lses)
        expand = np.zeros((LANES, D_MODEL), np.float32)
        for hd in range(N_HEADS):
            expand[hd, hd * HEAD_DIM:(hd + 1) * HEAD_DIM] = 1.0
        in_specs.append(pl.BlockSpec((LANES, D_MODEL), fix))
        args.append(jnp.asarray(expand, BF16))
        scratch = [pltpu.VMEM((n, N_CHUNKS, tm, LANES), F32), pltpu.VMEM((n, tm, LANES), F32)]
    in_specs += [pl.BlockSpec((D_MODEL, D_MODEL), fix), pl.BlockSpec((1, D_MODEL), fix),
                 pl.BlockSpec((1, tm, D_MODEL), tok), pl.BlockSpec((1, D_MODEL), fix), pl.BlockSpec((1, D_MODEL), fix)]
    args += [w, b, x, g, beta]
    return pl.pallas_call(
        functools.partial(_oproj_ln_kernel, dils=tuple(dils), tm=tm),
        grid=(bsz, seq // tm),
        in_specs=in_specs,
        out_specs=[pl.BlockSpec((1, tm, D_MODEL), tok), pl.BlockSpec((1, tm, D_MODEL), tok)],
        out_shape=[jax.ShapeDtypeStruct((bsz, seq, D_MODEL), F32), jax.ShapeDtypeStruct((bsz, seq, D_MODEL), BF16)],
        scratch_shapes=scratch,
        compiler_params=_params("parallel", "parallel"),
        name="oproj_ln",
    )(*args)


FFN_ROWS = 256


def _swiglu_ln(x, xb, wg_ref, wu_ref, wd_ref, g_ref, beta_ref):
    gate = jnp.dot(xb, wg_ref[...], preferred_element_type=F32)
    up = jnp.dot(xb, wu_ref[...], preferred_element_type=F32)
    hid = (gate * jax.nn.sigmoid(gate) * up).astype(BF16)
    y = jnp.dot(hid, wd_ref[...], preferred_element_type=F32)
    return _layer_norm(DN_ALPHA * x + y, g_ref[...], beta_ref[...])


def _ffn_ln_kernel(x_ref, xb_ref, wg_ref, wu_ref, wd_ref, g_ref, beta_ref, out_ref, outb_ref):
    for r in range(x_ref.shape[0] // FFN_ROWS):
        rows = slice(r * FFN_ROWS, (r + 1) * FFN_ROWS)
        y = _swiglu_ln(x_ref[rows, :], xb_ref[rows, :], wg_ref, wu_ref, wd_ref, g_ref, beta_ref)
        out_ref[rows, :] = y
        outb_ref[rows, :] = y.astype(BF16)


def _oproj_ffn_kernel(o_ref, wo_ref, bo_ref, x_ref, g0_ref, beta0_ref, wg_ref, wu_ref, wd_ref, g1_ref, beta1_ref,
                      out_ref, outb_ref):
    for r in range(x_ref.shape[0] // FFN_ROWS):
        rows = slice(r * FFN_ROWS, (r + 1) * FFN_ROWS)
        h = jnp.dot(o_ref[rows, :], wo_ref[...], preferred_element_type=F32) + bo_ref[...]
        x1 = _layer_norm(DN_ALPHA * x_ref[rows, :] + h, g0_ref[...], beta0_ref[...])
        y = _swiglu_ln(x1, x1.astype(BF16), wg_ref, wu_ref, wd_ref, g1_ref, beta1_ref)
        out_ref[rows, :] = y
        outb_ref[rows, :] = y.astype(BF16)


def _oproj_ffn(o, wo, bo, x, g0, beta0, wg, wu, wd, g1, beta1, *, tm=512):
    t = x.shape[0]
    tm = min(tm, t)
    row = lambda i: (i, 0)
    fix = lambda i: (0, 0)
    resident = dict(pipeline_mode=pl.Buffered(1))
    vec = pl.BlockSpec((1, D_MODEL), fix)
    return pl.pallas_call(
        _oproj_ffn_kernel,
        grid=(t // tm,),
        in_specs=[pl.BlockSpec((tm, D_MODEL), row), pl.BlockSpec((D_MODEL, D_MODEL), fix, **resident), vec,
                  pl.BlockSpec((tm, D_MODEL), row), vec, vec,
                  pl.BlockSpec((D_MODEL, D_FF), fix, **resident),
                  pl.BlockSpec((D_MODEL, D_FF), fix, **resident),
                  pl.BlockSpec((D_FF, D_MODEL), fix, **resident), vec, vec],
        out_specs=[pl.BlockSpec((tm, D_MODEL), row), pl.BlockSpec((tm, D_MODEL), row)],
        out_shape=[jax.ShapeDtypeStruct((t, D_MODEL), F32), jax.ShapeDtypeStruct((t, D_MODEL), BF16)],
        compiler_params=_params("parallel"),
        name="oproj_ffn",
    )(o, wo, bo, x, g0, beta0, wg, wu, wd, g1, beta1)


def _ffn_ln(x, xb, wg, wu, wd, g, beta, *, tm=1024):
    t = x.shape[0]
    tm = min(tm, t)
    row = lambda i: (i, 0)
    fix = lambda i: (0, 0)
    resident = dict(pipeline_mode=pl.Buffered(1))
    return pl.pallas_call(
        _ffn_ln_kernel,
        grid=(t // tm,),
        in_specs=[pl.BlockSpec((tm, D_MODEL), row), pl.BlockSpec((tm, D_MODEL), row),
                  pl.BlockSpec((D_MODEL, D_FF), fix, **resident),
                  pl.BlockSpec((D_MODEL, D_FF), fix, **resident),
                  pl.BlockSpec((D_FF, D_MODEL), fix, **resident),
                  pl.BlockSpec((1, D_MODEL), fix), pl.BlockSpec((1, D_MODEL), fix)],
        out_specs=[pl.BlockSpec((tm, D_MODEL), row), pl.BlockSpec((tm, D_MODEL), row)],
        out_shape=[jax.ShapeDtypeStruct((t, D_MODEL), F32), jax.ShapeDtypeStruct((t, D_MODEL), BF16)],
        compiler_params=_params("parallel"),
        name="ffn_ln",
    )(x, xb, wg, wu, wd, g, beta)


def kernel(x, positions, ln_g, ln_b, a_w_qkv, a_b_qkv, a_sinks, a_w_o, a_b_o, b_w_qkv, b_w_o, c_w_qkv, c_w_o,
           w_gate_up, w_down):
    bsz, seq, d = x.shape
    assert d == D_MODEL and seq % (B_GROUPS[-1][1] * TILE) == 0 and seq % MOBA_BLOCK == 0
    t = bsz * seq
    tables = _rope_tables(positions.reshape(t, 1))
    xf, xb = x, x.astype(BF16)
    zero_bias = jnp.zeros((1, D_MODEL), F32)
    zero_bias3 = jnp.zeros((1, 3 * D_MODEL), F32)
    row = lambda v: v.reshape(1, -1)
    for i in range(DEPTH):
        kind, j = i % 3, i // 3
        g0, b0, g1, b1 = row(ln_g[i, 0]), row(ln_b[i, 0]), row(ln_g[i, 1]), row(ln_b[i, 1])
        wgu = w_gate_up[i]
        ffn_w = (wgu[:, :D_FF].astype(BF16), wgu[:, D_FF:].astype(BF16), w_down[i].astype(BF16))
        if kind == 1:
            w = b_w_qkv[j].astype(BF16)
            os_, lses, dils = [], [], []
            for gi, (window, dil) in enumerate(B_GROUPS):
                cols = slice(3 * gi * D_MODEL, 3 * (gi + 1) * D_MODEL)
                ln = seq // dil
                if dil == 1:
                    qk, vt = _qkv_proj(xb, w[:, cols], zero_bias3, tables)
                    qk = qk[:, 0]
                else:
                    qk, vt = _qkv_proj(xb, w[:, cols], zero_bias3, tables, dil=dil)[0], None
                    qk = qk.reshape(bsz * dil, ln, 3 * D_MODEL)
                o, lse = _band_attention(qk, vt, n_back=window // dil, with_lse=True)
                os_.append(o.reshape(bsz, dil, ln, D_MODEL))
                lses.append(lse.reshape(bsz, dil, ln, LANES))
                dils.append(dil)
            xf, xb = _oproj_ln(os_, lses, dils, b_w_o[j].astype(BF16), zero_bias, xf, g0, b0)
            y, yb = _ffn_ln(xf.reshape(t, d), xb.reshape(t, d), *ffn_w, g1, b1)
        else:
            if kind == 0:
                qk, vt = _qkv_proj(xb, a_w_qkv[j].astype(BF16), row(a_b_qkv[j]), tables)
                o = _band_attention(qk[:, 0], vt, n_back=A_WINDOW - 1, sinks=a_sinks[j])[0]
                wo, bo = a_w_o[j], row(a_b_o[j])
            else:
                qk, vt, kmean = _qkv_proj(xb, c_w_qkv[j].astype(BF16), zero_bias3, tables, with_kmean=True)
                o = _moba_attention(qk[:, 0], vt, kmean[:, :, 0])
                wo, bo = c_w_o[j], zero_bias
            y, yb = _oproj_ffn(o.reshape(t, d), wo.astype(BF16), bo, xf.reshape(t, d), g0, b0, *ffn_w, g1, b1)
        xf, xb = y.reshape(bsz, seq, d), yb.reshape(bsz, seq, d)
    return xf
```

```python
import functools
import math

import numpy as np
import jax
import jax.numpy as jnp
from jax import lax
from jax.experimental import pallas as pl
from jax.experimental.pallas import tpu as pltpu

D_MODEL = 1024
N_HEADS = 16
HEAD_DIM = 64
ROT_DIM = HEAD_DIM // 4
ROPE_THETA = 500000.0
A_KV_HEADS = 4
A_WINDOW = 128
B_GROUPS = ((128, 1), (512, 4), (2048, 16))
MOBA_BLOCK = 256
MOBA_TOPK = 3
D_FF = 2816
DEPTH = 4
DN_ALPHA = (2 * DEPTH) ** 0.25
LN_EPS = 1e-5
NEG = -1e30

LANES = 128
TILE = 128
HEAD_PAIRS = N_HEADS // 2
N_CHUNKS = D_MODEL // LANES
VMEM_LIMIT = 56 * 1024 * 1024
LOG2E = math.log2(math.e)
LN2 = math.log(2.0)
Q_SCALE = HEAD_DIM ** -0.5 * LOG2E

F32 = jnp.float32
BF16 = jnp.bfloat16
_NT = (((1,), (1,)), ((), ()))
_TN = (((0,), (0,)), ((), ()))


def _params(*sem):
    return pltpu.CompilerParams(dimension_semantics=sem, vmem_limit_bytes=VMEM_LIMIT)


_INV_FREQ = [float(np.float32(ROPE_THETA ** (-(2.0 * i) / ROT_DIM))) for i in range(ROT_DIM // 2)]


def _rope_table_kernel(pos_ref, cos_ref, sp_ref, sm_ref):
    pos = pos_ref[...].astype(F32)
    lane = lax.broadcasted_iota(jnp.int32, (1, LANES), 1)
    c = lane & (HEAD_DIM - 1)
    f = c & (ROT_DIM // 2 - 1)
    inv = jnp.zeros((1, LANES), F32)
    for i, v in enumerate(_INV_FREQ):
        inv = jnp.where(f == i, v, inv)
    ang = pos * inv
    cs, sn = jnp.cos(ang), jnp.sin(ang)
    cos_ref[...] = jnp.where(c < ROT_DIM, cs, 1.0)
    sp_ref[...] = jnp.where((c >= ROT_DIM // 2) & (c < ROT_DIM), sn, 0.0)
    sm_ref[...] = jnp.where(c < ROT_DIM // 2, -sn, 0.0)


def _rope_tables(pos_col):
    t = pos_col.shape[0]
    tm = min(t, 2048)
    spec = pl.BlockSpec((tm, LANES), lambda i: (i, 0))
    shp = jax.ShapeDtypeStruct((t, LANES), F32)
    return pl.pallas_call(
        _rope_table_kernel,
        grid=(t // tm,),
        in_specs=[pl.BlockSpec((tm, 1), lambda i: (i, 0))],
        out_specs=[spec, spec, spec],
        out_shape=[shp, shp, shp],
        compiler_params=_params("parallel"),
        name="rope_tables",
    )(pos_col)


PROJ_ROWS = 256


def _rotary(y, cs, sp, sm):
    return y * cs + pltpu.roll(y, ROT_DIM // 2, 1) * sp + pltpu.roll(y, LANES - ROT_DIM // 2, 1) * sm


def _proj_shared_kv(xr, w_ref, b_ref, tabs, qk_ref, vt_ref, kind, r):
    kv = A_KV_HEADS * HEAD_DIM
    base = D_MODEL + (kind - 1) * kv
    acc = jnp.dot(xr, w_ref[:, base:base + kv], preferred_element_type=F32) + b_ref[:, base:base + kv]
    lane = lax.broadcasted_iota(jnp.int32, (PROJ_ROWS, LANES), 1)
    rows = slice(r * PROJ_ROWS, (r + 1) * PROJ_ROWS)
    pairs_per_kv = N_HEADS // A_KV_HEADS // 2
    for cc in range(kv // LANES):
        y = acc[:, cc * LANES:(cc + 1) * LANES]
        if kind == 1:
            y = _rotary(y, *tabs)
        swapped = pltpu.roll(y, HEAD_DIM, 1)
        for par, both in enumerate((jnp.where(lane < HEAD_DIM, y, swapped), jnp.where(lane < HEAD_DIM, swapped, y))):
            g = 2 * cc + par
            if kind == 2:
                tiles = [both[tl * TILE:(tl + 1) * TILE].T.astype(BF16) for tl in range(PROJ_ROWS // TILE)]
            for hp in range(g * pairs_per_kv, (g + 1) * pairs_per_kv):
                sl = slice(hp * LANES, (hp + 1) * LANES)
                if kind == 1:
                    qk_ref[0, 0, rows, D_MODEL + hp * LANES:D_MODEL + (hp + 1) * LANES] = both.astype(BF16)
                else:
                    for tl, tile in enumerate(tiles):
                        vt_ref[0, r * (PROJ_ROWS // TILE) + tl, sl, :] = tile


def _proj_kernel(x_ref, w_ref, b_ref, cos_ref, sp_ref, sm_ref, *rest, tm, dil, with_kmean, gqa):
    rest = list(rest)
    perm_ref = rest.pop(0) if dil > 1 else None
    qk_ref = rest.pop(0)
    vt_ref = rest.pop(0) if dil == 1 else None
    km_ref = rest.pop(0) if with_kmean else None
    sub = PROJ_ROWS // dil
    for r in range(tm // PROJ_ROWS):
        rows = slice(r * PROJ_ROWS, (r + 1) * PROJ_ROWS)
        xr = x_ref[0, rows, :].astype(BF16)
        if dil == 1:
            tabs = (cos_ref[0, rows, :], sp_ref[0, rows, :], sm_ref[0, rows, :])
        else:
            xr = jnp.dot(perm_ref[...], xr, preferred_element_type=F32).astype(BF16)
            tabs = tuple(
                jnp.concatenate([tb.at[0][pl.ds(r * PROJ_ROWS + ph, sub, stride=dil), :] for ph in range(dil)], axis=0)
                for tb in (cos_ref, sp_ref, sm_ref))
        for kind in range(3):
            if gqa and kind > 0:
                _proj_shared_kv(xr, w_ref, b_ref, tabs, qk_ref, vt_ref, kind, r)
                continue
            cols = slice(kind * D_MODEL, (kind + 1) * D_MODEL)
            acc = jnp.dot(xr, w_ref[:, cols], preferred_element_type=F32) + b_ref[:, cols]
            cs, sp, sm = [tb * Q_SCALE for tb in tabs] if kind == 0 else tabs
            for c in range(N_CHUNKS):
                sl = slice(c * LANES, (c + 1) * LANES)
                osl = slice(kind * D_MODEL + c * LANES, kind * D_MODEL + (c + 1) * LANES)
                y = acc[:, sl]
                if kind < 2:
                    y = _rotary(y, cs, sp, sm)
                if dil > 1:
                    yb = y.astype(BF16)
                    for ph in range(dil):
                        qk_ref[0, ph, r * sub:(r + 1) * sub, osl] = yb[ph * sub:(ph + 1) * sub]
                elif kind < 2:
                    qk_ref[0, 0, rows, osl] = y.astype(BF16)
                else:
                    for tl in range(PROJ_ROWS // TILE):
                        vt_ref[0, r * (PROJ_ROWS // TILE) + tl, sl, :] = y[tl * TILE:(tl + 1) * TILE].T.astype(BF16)
                if with_kmean and kind == 1:
                    km_ref[0, r, :, sl] = jnp.sum(y, axis=0, keepdims=True) * (1.0 / MOBA_BLOCK)


def _qkv_proj(xb, w, b, tables, *, dil=1, with_kmean=False, tm=512):
    bsz, seq, _ = xb.shape
    tm = min(tm, seq)
    gqa = w.shape[1] == D_MODEL + 2 * A_KV_HEADS * HEAD_DIM
    assert gqa or w.shape[1] == 3 * D_MODEL
    assert not gqa or (dil == 1 and not with_kmean)
    assert PROJ_ROWS == MOBA_BLOCK and tm % PROJ_ROWS == 0 and PROJ_ROWS % (16 * dil) == 0
    tabs = [tb.reshape(bsz, seq, LANES) for tb in tables]
    tab_spec = pl.BlockSpec((1, tm, LANES), lambda bi, i: (bi, i, 0))
    fix = lambda bi, i: (0, 0)
    once = dict(pipeline_mode=pl.Buffered(1))
    ncol = 2 * D_MODEL if dil == 1 else 3 * D_MODEL
    out_specs = [pl.BlockSpec((1, dil, tm // dil, ncol), lambda bi, i: (bi, 0, i, 0))]
    out_shape = [jax.ShapeDtypeStruct((bsz, dil, seq // dil, ncol), BF16)]
    if dil == 1:
        out_specs.append(pl.BlockSpec((1, tm // TILE, D_MODEL, TILE), lambda bi, i: (bi, i, 0, 0)))
        out_shape.append(jax.ShapeDtypeStruct((bsz, seq // TILE, D_MODEL, TILE), BF16))
    if with_kmean:
        out_specs.append(pl.BlockSpec((1, tm // MOBA_BLOCK, 1, D_MODEL), lambda bi, i: (bi, i, 0, 0)))
        out_shape.append(jax.ShapeDtypeStruct((bsz, seq // MOBA_BLOCK, 1, D_MODEL), F32))
    in_specs = [
        pl.BlockSpec((1, tm, D_MODEL), lambda bi, i: (bi, i, 0)),
        pl.BlockSpec((D_MODEL, w.shape[1]), fix, **once),
        pl.BlockSpec((1, w.shape[1]), fix),
        tab_spec, tab_spec, tab_spec,
    ]
    args = [xb, w, b, *tabs]
    if dil > 1:
        p = np.arange(PROJ_ROWS)
        perm = np.zeros((PROJ_ROWS, PROJ_ROWS), np.float32)
        perm[p, (p % (PROJ_ROWS // dil)) * dil + p // (PROJ_ROWS // dil)] = 1.0
        in_specs.append(pl.BlockSpec((PROJ_ROWS, PROJ_ROWS), fix))
        args.append(jnp.asarray(perm, BF16))
    return pl.pallas_call(
        functools.partial(_proj_kernel, tm=tm, dil=dil, with_kmean=with_kmean, gqa=gqa),
        grid=(bsz, seq // tm),
        in_specs=in_specs,
        out_specs=out_specs,
        out_shape=out_shape,
        compiler_params=_params("parallel", "parallel"),
        name="qkv_proj",
    )(*args)


def _band_bias(n_back):
    k = np.arange(TILE)[:, None]
    q = (np.arange(2 * TILE) % TILE)[None, :]
    prev = q + TILE - k <= n_back
    own = k <= q
    return jnp.asarray(np.where(np.stack([prev, np.zeros_like(prev), own]), 0.0, NEG), F32)


def _stack_heads(q):
    lane = lax.broadcasted_iota(jnp.int32, q.shape, 1)
    zero = jnp.zeros_like(q)
    return jnp.concatenate([jnp.where(lane < HEAD_DIM, q, zero), jnp.where(lane >= HEAD_DIM, q, zero)], axis=0)


BAND_QBLOCKS = 8


def _band_attn_kernel(*refs, with_sink, with_lse, v_transposed, nq):
    refs = list(refs)
    sink_ref = refs.pop(0) if with_sink else None
    q_ref, kp_ref, kc_ref, vp_ref, vc_ref, bias_ref, o_ref = refs[:7]
    lse_ref = refs[7] if with_lse else None
    first = (pl.program_id(1) == 0).astype(jnp.int32)
    col = lax.broadcasted_iota(jnp.int32, (1, 2 * TILE), 1)
    for t in range(nq):
        rows = slice(t * TILE, (t + 1) * TILE)
        prow = slice((t - 1) * TILE, t * TILE)
        lse_rows = []
        for hp in range(HEAD_PAIRS):
            sl = slice(hp * LANES, (hp + 1) * LANES)
            qs = _stack_heads(q_ref[0, rows, sl])
            if t == 0:
                k_prev, bias_prev = kp_ref[0, :, sl], bias_ref[first]
                v_prev = vp_ref[0, 0, sl, :] if v_transposed else vp_ref[0, :, sl]
            else:
                k_prev, bias_prev = kc_ref[0, prow, sl], bias_ref[0]
                v_prev = vc_ref[0, t - 1, sl, :] if v_transposed else vc_ref[0, prow, sl]
            v_own = vc_ref[0, t, sl, :] if v_transposed else vc_ref[0, rows, sl]
            parts = []
            for k, v, bias in ((k_prev, v_prev, bias_prev), (kc_ref[0, rows, sl], v_own, bias_ref[2])):
                s = lax.dot_general(k, qs, _NT, preferred_element_type=F32) + bias
                m = jnp.max(s, axis=0, keepdims=True)
                p = jnp.exp2(s - m)
                l = jnp.sum(p, axis=0, keepdims=True)
                if v_transposed:
                    pv = jnp.dot(v, p.astype(BF16), preferred_element_type=F32)
                else:
                    pv = lax.dot_general(v, p.astype(BF16), _TN, preferred_element_type=F32)
                parts.append((m, l, pv))
            (m0, l0, pv0), (m1, l1, pv1) = parts
            m = jnp.maximum(m0, m1)
            if with_sink:
                sk = jnp.where(col < TILE, sink_ref[2 * hp], sink_ref[2 * hp + 1]) * LOG2E
                m = jnp.maximum(m, sk)
            a0, a1 = jnp.exp2(m0 - m), jnp.exp2(m1 - m)
            l = a0 * l0 + a1 * l1
            if with_sink:
                l = l + jnp.exp2(sk - m)
            inv = 1.0 / l
            w0, w1 = a0 * inv, a1 * inv
            o_t = jnp.concatenate(
                [pv0[:HEAD_DIM, :TILE] * w0[:, :TILE] + pv1[:HEAD_DIM, :TILE] * w1[:, :TILE],
                 pv0[HEAD_DIM:, TILE:] * w0[:, TILE:] + pv1[HEAD_DIM:, TILE:] * w1[:, TILE:]], axis=0)
            o_ref[0, rows, sl] = o_t.T.astype(BF16)
            if with_lse:
                lse = (m + jnp.log2(l)) * LN2
                lse_rows += [lse[:, :TILE], lse[:, TILE:]]
        if with_lse:
            pad = jnp.zeros((LANES - N_HEADS, TILE), F32)
            lse_ref[0, rows, :] = jnp.concatenate(lse_rows + [pad], axis=0).T


def _band_attention(qk, vt, *, n_back, sinks=None, with_lse=False):
    nseq, ln, _ = qk.shape
    nq = math.gcd(BAND_QBLOCKS, ln // TILE)
    blk = (1, nq * TILE, D_MODEL)
    cur = lambda c: pl.BlockSpec(blk, lambda s, i: (s, i, c))
    prev = lambda c: pl.BlockSpec((1, TILE, D_MODEL), lambda s, i: (s, jnp.maximum(i * nq - 1, 0), c))
    if vt is None:
        v_specs, v_args = [prev(2), cur(2)], [qk, qk]
    else:
        v_specs = [pl.BlockSpec((1, 1, D_MODEL, TILE), lambda s, i: (s, jnp.maximum(i * nq - 1, 0), 0, 0)),
                   pl.BlockSpec((1, nq, D_MODEL, TILE), lambda s, i: (s, i, 0, 0))]
        v_args = [vt, vt]
    in_specs = [cur(0), prev(1), cur(1)] + v_specs + [pl.BlockSpec((3, TILE, 2 * TILE), lambda s, i: (0, 0, 0))]
    args = [qk] * 3 + v_args + [_band_bias(n_back)]
    if sinks is not None:
        in_specs.insert(0, pl.BlockSpec(memory_space=pltpu.SMEM))
        args.insert(0, sinks)
    out_specs = [pl.BlockSpec(blk, lambda s, i: (s, i, 0))]
    out_shape = [jax.ShapeDtypeStruct((nseq, ln, D_MODEL), BF16)]
    if with_lse:
        out_specs.append(pl.BlockSpec((1, nq * TILE, LANES), lambda s, i: (s, i, 0)))
        out_shape.append(jax.ShapeDtypeStruct((nseq, ln, LANES), F32))
    return pl.pallas_call(
        functools.partial(_band_attn_kernel, with_sink=sinks is not None, with_lse=with_lse,
                          v_transposed=vt is not None, nq=nq),
        grid=(nseq, ln // (nq * TILE)),
        in_specs=in_specs,
        out_specs=out_specs,
        out_shape=out_shape,
        compiler_params=_params("parallel", "arbitrary"),
        name="band_attn",
    )(*args)


N_QSUB = MOBA_BLOCK // TILE
N_CHAINS = HEAD_PAIRS * N_QSUB
PAST_BLOCKS = 4


def _moba_kernel(q_ref, k_ref, vt_ref, km_ref, o_ref, qs_ref, sel_ref, m_ref, l_ref, acc_ref, *, nblk):
    qb = pl.program_id(1)
    blk = lax.broadcasted_iota(jnp.int32, (nblk, MOBA_BLOCK), 0)
    blkf = blk.astype(F32)
    lane = lax.broadcasted_iota(jnp.int32, (MOBA_BLOCK, LANES), 1)
    kidx = lax.broadcasted_iota(jnp.int32, (TILE, 2 * TILE), 0)
    qidx = lax.broadcasted_iota(jnp.int32, (TILE, 2 * TILE), 1) & (TILE - 1)
    causal = jnp.where(kidx <= qidx, 0.0, NEG).astype(F32)

    for hp in range(HEAD_PAIRS):
        sl = slice(hp * LANES, (hp + 1) * LANES)
        km = km_ref[0, :, sl]
        km_hi = km.astype(BF16)
        km_lo = (km - km_hi.astype(F32)).astype(BF16)
        for qs in range(N_QSUB):
            c = hp * N_QSUB + qs
            stacked = _stack_heads(q_ref[0, qs * TILE:(qs + 1) * TILE, sl])
            qs_ref[c] = stacked
            gate = (lax.dot_general(km_hi, stacked, _NT, preferred_element_type=F32)
                    + lax.dot_general(km_lo, stacked, _NT, preferred_element_type=F32))
            g = jnp.where(blk < qb, gate, NEG)
            sel = jnp.zeros((nblk, 2 * TILE), F32)
            for r in range(min(MOBA_TOPK, nblk)):
                mx = jnp.max(g, axis=0, keepdims=True)
                idx = jnp.min(jnp.where(g == mx, blkf, float(nblk)), axis=0, keepdims=True)
                hit = blkf == idx
                sel = jnp.where(hit & (r < qb), 1.0, sel)
                g = jnp.where(hit, -jnp.inf, g)
            sel_ref[c] = sel
    m_ref[...] = jnp.full(m_ref.shape, NEG, F32)
    l_ref[...] = jnp.zeros(l_ref.shape, F32)
    acc_ref[...] = jnp.zeros(acc_ref.shape, F32)

    def update(c, hp, jt, bias_row, bias_tile):
        sl = slice(hp * LANES, (hp + 1) * LANES)
        off = pl.multiple_of(jt * TILE, TILE)
        s = lax.dot_general(k_ref[0, pl.ds(off, TILE), sl], qs_ref[c], _NT, preferred_element_type=F32)
        if bias_tile is not None:
            s = s + bias_tile
        mj = jnp.max(s, axis=0, keepdims=True)
        m_old = m_ref[c]
        if bias_row is None:
            m_new = jnp.maximum(m_old, mj)
            p = jnp.exp2(s - m_new)
        else:
            m_new = jnp.maximum(m_old, jnp.where(bias_row, mj, NEG))
            p = jnp.exp2(s + jnp.where(bias_row, -m_new, NEG))
        alpha = jnp.exp2(m_old - m_new)
        m_ref[c] = m_new
        l_ref[c] = alpha * l_ref[c] + jnp.sum(p, axis=0, keepdims=True)
        pv = jnp.dot(vt_ref[0, jt, sl, :], p.astype(BF16), preferred_element_type=F32)
        acc_ref[c] = alpha * acc_ref[c] + jnp.concatenate([pv[:HEAD_DIM, :TILE], pv[HEAD_DIM:, TILE:]], axis=1)

    def past_blocks(j0, nb):
        for u in range(nb * N_QSUB):
            j = j0 + u // N_QSUB
            for c in range(N_CHAINS):
                update(c, c // N_QSUB, j0 * N_QSUB + u, sel_ref[c, pl.ds(j, 1), :] > 0.5, None)

    def past(jj, carry):
        past_blocks(jj * PAST_BLOCKS, PAST_BLOCKS)
        return carry

    lax.fori_loop(0, qb // PAST_BLOCKS, past, 0)
    for rem in range(1, PAST_BLOCKS):
        @pl.when(qb % PAST_BLOCKS >= rem)
        def _():
            past_blocks(qb - qb % PAST_BLOCKS + rem - 1, 1)

    for c in range(N_CHAINS):
        hp, qs = c // N_QSUB, c % N_QSUB
        for kt in range(qs + 1):
            update(c, hp, qb * N_QSUB + kt, None, causal if kt == qs else None)

    for c in range(N_CHAINS):
        hp, qs = c // N_QSUB, c % N_QSUB
        o = acc_ref[c] * (1.0 / l_ref[c])
        o_t = jnp.concatenate([o[:, :TILE], o[:, TILE:]], axis=0)
        o_ref[0, qs * TILE:(qs + 1) * TILE, hp * LANES:(hp + 1) * LANES] = o_t.T.astype(BF16)


def _moba_attention(qk, vt, kmean):
    bsz, seq, _ = qk.shape
    nblk = seq // MOBA_BLOCK
    whole = dict(pipeline_mode=pl.Buffered(1))
    return pl.pallas_call(
        functools.partial(_moba_kernel, nblk=nblk),
        grid=(bsz, nblk),
        in_specs=[
            pl.BlockSpec((1, MOBA_BLOCK, D_MODEL), lambda b, i: (b, i, 0)),
            pl.BlockSpec((1, seq, D_MODEL), lambda b, i: (b, 0, 1), **whole),
            pl.BlockSpec((1, seq // TILE, D_MODEL, TILE), lambda b, i: (b, 0, 0, 0), **whole),
            pl.BlockSpec((1, nblk, D_MODEL), lambda b, i: (b, 0, 0)),
        ],
        out_specs=pl.BlockSpec((1, MOBA_BLOCK, D_MODEL), lambda b, i: (b, i, 0)),
        out_shape=jax.ShapeDtypeStruct((bsz, seq, D_MODEL), BF16),
        scratch_shapes=[pltpu.VMEM((N_CHAINS, 2 * TILE, LANES), BF16),
                        pltpu.VMEM((N_CHAINS, nblk, 2 * TILE), F32),
                        pltpu.VMEM((N_CHAINS, 1, 2 * TILE), F32),
                        pltpu.VMEM((N_CHAINS, 1, 2 * TILE), F32),
                        pltpu.VMEM((N_CHAINS, HEAD_DIM, 2 * TILE), F32)],
        compiler_params=_params("parallel", "arbitrary"),
        name="moba_attn",
    )(qk, qk, vt, kmean)


def _layer_norm(y, g, b):
    mu = jnp.mean(y, axis=-1, keepdims=True)
    d = y - mu
    var = jnp.mean(d * d, axis=-1, keepdims=True)
    return d * lax.rsqrt(var + LN_EPS) * g + b


def _oproj_ln_kernel(*refs, dils, tm):
    n = len(dils)
    o_refs, refs = refs[:n], refs[n:]
    if n > 1:
        lse_refs, e_ref, refs = refs[:n], refs[n], refs[n + 1:]
    w_ref, b_ref, x_ref, g_ref, beta_ref, out_ref, outb_ref = refs[:7]
    scr = refs[7:]
    if n == 1:
        o = o_refs[0][0, 0]
    else:
        o_scr, lse_scr = scr
        lses = []
        for gi, d in enumerate(dils):
            if d == 1:
                lses.append(lse_refs[gi][0, 0])
            else:
                for ph in range(d):
                    lse_scr.at[gi][pl.ds(ph, tm // d, stride=d), :] = lse_refs[gi][0, ph]
                lses.append(lse_scr[gi])
        mx = functools.reduce(jnp.maximum, lses)
        es = [jnp.exp(l - mx) for l in lses]
        inv = 1.0 / functools.reduce(lambda a, c: a + c, es)
        chunks = [None] * N_CHUNKS
        for gi, d in enumerate(dils):
            wgt = es[gi] * inv
            hi = wgt.astype(BF16)
            lo = (wgt - hi.astype(F32)).astype(BF16)
            wide = (jnp.dot(hi, e_ref[...], preferred_element_type=F32)
                    + jnp.dot(lo, e_ref[...], preferred_element_type=F32))
            for c in range(N_CHUNKS):
                sl = slice(c * LANES, (c + 1) * LANES)
                if d == 1:
                    og = o_refs[gi][0, 0, :, sl].astype(F32)
                else:
                    for ph in range(d):
                        o_scr.at[gi, c][pl.ds(ph, tm // d, stride=d), :] = o_refs[gi][0, ph, :, sl].astype(F32)
                    og = o_scr[gi, c]
                term = wide[:, sl] * og
                chunks[c] = term if chunks[c] is None else chunks[c] + term
        o = jnp.concatenate(chunks, axis=1).astype(BF16)
    h = jnp.dot(o, w_ref[...], preferred_element_type=F32) + b_ref[...]
    y = _layer_norm(DN_ALPHA * x_ref[0] + h, g_ref[...], beta_ref[...])
    out_ref[0] = y
    outb_ref[0] = y.astype(BF16)


def _oproj_ln(os_, lses, dils, w, b, x, g, beta, *, tm=512):
    bsz, seq, _ = x.shape
    tm = min(tm, seq)
    n = len(os_)
    tok = lambda bi, i: (bi, i, 0)
    ph = lambda bi, i: (bi, 0, i, 0)
    fix = lambda bi, i: (0, 0)
    in_specs = [pl.BlockSpec((1, d, tm // d, D_MODEL), ph) for d in dils]
    args = list(os_)
    scratch = []
    if n > 1:
        in_specs += [pl.BlockSpec((1, d, tm // d, LANES), ph) for d in dils]
        args += list(lses)
        expand = np.zeros((LANES, D_MODEL), np.float32)
        for hd in range(N_HEADS):
            expand[hd, hd * HEAD_DIM:(hd + 1) * HEAD_DIM] = 1.0
        in_specs.append(pl.BlockSpec((LANES, D_MODEL), fix))
        args.append(jnp.asarray(expand, BF16))
        scratch = [pltpu.VMEM((n, N_CHUNKS, tm, LANES), F32), pltpu.VMEM((n, tm, LANES), F32)]
    in_specs += [pl.BlockSpec((D_MODEL, D_MODEL), fix), pl.BlockSpec((1, D_MODEL), fix),
                 pl.BlockSpec((1, tm, D_MODEL), tok), pl.BlockSpec((1, D_MODEL), fix), pl.BlockSpec((1, D_MODEL), fix)]
    args += [w, b, x, g, beta]
    return pl.pallas_call(
        functools.partial(_oproj_ln_kernel, dils=tuple(dils), tm=tm),
        grid=(bsz, seq // tm),
        in_specs=in_specs,
        out_specs=[pl.BlockSpec((1, tm, D_MODEL), tok), pl.BlockSpec((1, tm, D_MODEL), tok)],
        out_shape=[jax.ShapeDtypeStruct((bsz, seq, D_MODEL), F32), jax.ShapeDtypeStruct((bsz, seq, D_MODEL), BF16)],
        scratch_shapes=scratch,
        compiler_params=_params("parallel", "parallel"),
        name="oproj_ln",
    )(*args)


FFN_ROWS = 256


def _swiglu_ln(x, xb, wg_ref, wu_ref, wd_ref, g_ref, beta_ref):
    gate = jnp.dot(xb, wg_ref[...], preferred_element_type=F32)
    up = jnp.dot(xb, wu_ref[...], preferred_element_type=F32)
    hid = (gate * jax.nn.sigmoid(gate) * up).astype(BF16)
    y = jnp.dot(hid, wd_ref[...], preferred_element_type=F32)
    return _layer_norm(DN_ALPHA * x + y, g_ref[...], beta_ref[...])


def _ffn_ln_kernel(x_ref, xb_ref, wg_ref, wu_ref, wd_ref, g_ref, beta_ref, out_ref, outb_ref):
    for r in range(x_ref.shape[0] // FFN_ROWS):
        rows = slice(r * FFN_ROWS, (r + 1) * FFN_ROWS)
        y = _swiglu_ln(x_ref[rows, :], xb_ref[rows, :], wg_ref, wu_ref, wd_ref, g_ref, beta_ref)
        out_ref[rows, :] = y
        outb_ref[rows, :] = y.astype(BF16)


def _oproj_ffn_kernel(o_ref, wo_ref, bo_ref, x_ref, g0_ref, beta0_ref, wg_ref, wu_ref, wd_ref, g1_ref, beta1_ref,
                      out_ref, outb_ref):
    for r in range(x_ref.shape[0] // FFN_ROWS):
        rows = slice(r * FFN_ROWS, (r + 1) * FFN_ROWS)
        h = jnp.dot(o_ref[rows, :], wo_ref[...], preferred_element_type=F32) + bo_ref[...]
        x1 = _layer_norm(DN_ALPHA * x_ref[rows, :] + h, g0_ref[...], beta0_ref[...])
        y = _swiglu_ln(x1, x1.astype(BF16), wg_ref, wu_ref, wd_ref, g1_ref, beta1_ref)
        out_ref[rows, :] = y
        outb_ref[rows, :] = y.astype(BF16)


def _oproj_ffn(o, wo, bo, x, g0, beta0, wg, wu, wd, g1, beta1, *, tm=512):
    t = x.shape[0]
    tm = min(tm, t)
    row = lambda i: (i, 0)
    fix = lambda i: (0, 0)
    resident = dict(pipeline_mode=pl.Buffered(1))
    vec = pl.BlockSpec((1, D_MODEL), fix)
    return pl.pallas_call(
        _oproj_ffn_kernel,
        grid=(t // tm,),
        in_specs=[pl.BlockSpec((tm, D_MODEL), row), pl.BlockSpec((D_MODEL, D_MODEL), fix, **resident), vec,
                  pl.BlockSpec((tm, D_MODEL), row), vec, vec,
                  pl.BlockSpec((D_MODEL, D_FF), fix, **resident),
                  pl.BlockSpec((D_MODEL, D_FF), fix, **resident),
                  pl.BlockSpec((D_FF, D_MODEL), fix, **resident), vec, vec],
        out_specs=[pl.BlockSpec((tm, D_MODEL), row), pl.BlockSpec((tm, D_MODEL), row)],
        out_shape=[jax.ShapeDtypeStruct((t, D_MODEL), F32), jax.ShapeDtypeStruct((t, D_MODEL), BF16)],
        compiler_params=_params("parallel"),
        name="oproj_ffn",
    )(o, wo, bo, x, g0, beta0, wg, wu, wd, g1, beta1)


def _ffn_ln(x, xb, wg, wu, wd, g, beta, *, tm=1024):
    t = x.shape[0]
    tm = min(tm, t)
    row = lambda i: (i, 0)
    fix = lambda i: (0, 0)
    resident = dict(pipeline_mode=pl.Buffered(1))
    return pl.pallas_call(
        _ffn_ln_kernel,
        grid=(t // tm,),
        in_specs=[pl.BlockSpec((tm, D_MODEL), row), pl.BlockSpec((tm, D_MODEL), row),
                  pl.BlockSpec((D_MODEL, D_FF), fix, **resident),
                  pl.BlockSpec((D_MODEL, D_FF), fix, **resident),
                  pl.BlockSpec((D_FF, D_MODEL), fix, **resident),
                  pl.BlockSpec((1, D_MODEL), fix), pl.BlockSpec((1, D_MODEL), fix)],
        out_specs=[pl.BlockSpec((tm, D_MODEL), row), pl.BlockSpec((tm, D_MODEL), row)],
        out_shape=[jax.ShapeDtypeStruct((t, D_MODEL), F32), jax.ShapeDtypeStruct((t, D_MODEL), BF16)],
        compiler_params=_params("parallel"),
        name="ffn_ln",
    )(x, xb, wg, wu, wd, g, beta)


def kernel(x, positions, ln_g, ln_b, a_w_qkv, a_b_qkv, a_sinks, a_w_o, a_b_o, b_w_qkv, b_w_o, c_w_qkv, c_w_o,
           w_gate_up, w_down):
    bsz, seq, d = x.shape
    assert d == D_MODEL and seq % (B_GROUPS[-1][1] * TILE) == 0 and seq % MOBA_BLOCK == 0
    t = bsz * seq
    tables = _rope_tables(positions.reshape(t, 1))
    xf = xb = x
    zero_bias = jnp.zeros((1, D_MODEL), F32)
    zero_bias3 = jnp.zeros((1, 3 * D_MODEL), F32)
    row = lambda v: v.reshape(1, -1)
    for i in range(DEPTH):
        kind, j = i % 3, i // 3
        g0, b0, g1, b1 = row(ln_g[i, 0]), row(ln_b[i, 0]), row(ln_g[i, 1]), row(ln_b[i, 1])
        wgu = w_gate_up[i]
        ffn_w = (wgu[:, :D_FF].astype(BF16), wgu[:, D_FF:].astype(BF16), w_down[i].astype(BF16))
        if kind == 1:
            w = b_w_qkv[j].astype(BF16)
            os_, lses, dils = [], [], []
            for gi, (window, dil) in enumerate(B_GROUPS):
                cols = slice(3 * gi * D_MODEL, 3 * (gi + 1) * D_MODEL)
                ln = seq // dil
                if dil == 1:
                    qk, vt = _qkv_proj(xb, w[:, cols], zero_bias3, tables)
                    qk = qk[:, 0]
                else:
                    qk, vt = _qkv_proj(xb, w[:, cols], zero_bias3, tables, dil=dil)[0], None
                    qk = qk.reshape(bsz * dil, ln, 3 * D_MODEL)
                o, lse = _band_attention(qk, vt, n_back=window // dil, with_lse=True)
                os_.append(o.reshape(bsz, dil, ln, D_MODEL))
                lses.append(lse.reshape(bsz, dil, ln, LANES))
                dils.append(dil)
            xf, xb = _oproj_ln(os_, lses, dils, b_w_o[j].astype(BF16), zero_bias, xf, g0, b0)
            y, yb = _ffn_ln(xf.reshape(t, d), xb.reshape(t, d), *ffn_w, g1, b1)
        else:
            if kind == 0:
                qk, vt = _qkv_proj(xb, a_w_qkv[j].astype(BF16), row(a_b_qkv[j]), tables)
                o = _band_attention(qk[:, 0], vt, n_back=A_WINDOW - 1, sinks=a_sinks[j])[0]
                wo, bo = a_w_o[j], row(a_b_o[j])
            else:
                qk, vt, kmean = _qkv_proj(xb, c_w_qkv[j].astype(BF16), zero_bias3, tables, with_kmean=True)
                o = _moba_attention(qk[:, 0], vt, kmean[:, :, 0])
                wo, bo = c_w_o[j], zero_bias
            y, yb = _oproj_ffn(o.reshape(t, d), wo.astype(BF16), bo, xf.reshape(t, d), g0, b0, *ffn_w, g1, b1)
        xf, xb = y.reshape(bsz, seq, d), yb.reshape(bsz, seq, d)
    return xf
```

```python
import functools
import math

import numpy as np
import jax
import jax.numpy as jnp
from jax import lax
from jax.experimental import pallas as pl
from jax.experimental.pallas import tpu as pltpu

D_MODEL = 1024
N_HEADS = 16
HEAD_DIM = 64
ROT_DIM = HEAD_DIM // 4
ROPE_THETA = 500000.0
A_KV_HEADS = 4
A_WINDOW = 128
B_GROUPS = ((128, 1), (512, 4), (2048, 16))
MOBA_BLOCK = 256
MOBA_TOPK = 3
D_FF = 2816
DEPTH = 4
DN_ALPHA = (2 * DEPTH) ** 0.25
LN_EPS = 1e-5
NEG = -1e30

LANES = 128
TILE = 128
HEAD_PAIRS = N_HEADS // 2
N_CHUNKS = D_MODEL // LANES
VMEM_LIMIT = 56 * 1024 * 1024
LOG2E = math.log2(math.e)
LN2 = math.log(2.0)
Q_SCALE = HEAD_DIM ** -0.5 * LOG2E

F32 = jnp.float32
BF16 = jnp.bfloat16
_NT = (((1,), (1,)), ((), ()))
_TN = (((0,), (0,)), ((), ()))


def _params(*sem):
    return pltpu.CompilerParams(dimension_semantics=sem, vmem_limit_bytes=VMEM_LIMIT)


_INV_FREQ = [float(np.float32(ROPE_THETA ** (-(2.0 * i) / ROT_DIM))) for i in range(ROT_DIM // 2)]


def _rope_table_kernel(pos_ref, cos_ref, sp_ref, sm_ref):
    pos = pos_ref[...].astype(F32)
    lane = lax.broadcasted_iota(jnp.int32, (1, LANES), 1)
    c = lane & (HEAD_DIM - 1)
    f = c & (ROT_DIM // 2 - 1)
    inv = jnp.zeros((1, LANES), F32)
    for i, v in enumerate(_INV_FREQ):
        inv = jnp.where(f == i, v, inv)
    ang = pos * inv
    cs, sn = jnp.cos(ang), jnp.sin(ang)
    cos_ref[...] = jnp.where(c < ROT_DIM, cs, 1.0)
    sp_ref[...] = jnp.where((c >= ROT_DIM // 2) & (c < ROT_DIM), sn, 0.0)
    sm_ref[...] = jnp.where(c < ROT_DIM // 2, -sn, 0.0)


def _rope_tables(pos_col):
    t = pos_col.shape[0]
    tm = min(t, 2048)
    spec = pl.BlockSpec((tm, LANES), lambda i: (i, 0))
    shp = jax.ShapeDtypeStruct((t, LANES), F32)
    return pl.pallas_call(
        _rope_table_kernel,
        grid=(t // tm,),
        in_specs=[pl.BlockSpec((tm, 1), lambda i: (i, 0))],
        out_specs=[spec, spec, spec],
        out_shape=[shp, shp, shp],
        compiler_params=_params("parallel"),
        name="rope_tables",
    )(pos_col)


PROJ_ROWS = 256


def _rotary(y, cs, sp, sm):
    return y * cs + pltpu.roll(y, ROT_DIM // 2, 1) * sp + pltpu.roll(y, LANES - ROT_DIM // 2, 1) * sm


def _proj_shared_kv(xr, w_ref, b_ref, tabs, qk_ref, vt_ref, kind, r):
    kv = A_KV_HEADS * HEAD_DIM
    base = D_MODEL + (kind - 1) * kv
    acc = jnp.dot(xr, w_ref[:, base:base + kv], preferred_element_type=F32) + b_ref[:, base:base + kv]
    lane = lax.broadcasted_iota(jnp.int32, (PROJ_ROWS, LANES), 1)
    rows = slice(r * PROJ_ROWS, (r + 1) * PROJ_ROWS)
    pairs_per_kv = N_HEADS // A_KV_HEADS // 2
    for cc in range(kv // LANES):
        y = acc[:, cc * LANES:(cc + 1) * LANES]
        if kind == 1:
            y = _rotary(y, *tabs)
        swapped = pltpu.roll(y, HEAD_DIM, 1)
        for par, both in enumerate((jnp.where(lane < HEAD_DIM, y, swapped), jnp.where(lane < HEAD_DIM, swapped, y))):
            g = 2 * cc + par
            if kind == 2:
                tiles = [both[tl * TILE:(tl + 1) * TILE].T.astype(BF16) for tl in range(PROJ_ROWS // TILE)]
            for hp in range(g * pairs_per_kv, (g + 1) * pairs_per_kv):
                sl = slice(hp * LANES, (hp + 1) * LANES)
                if kind == 1:
                    qk_ref[0, 0, rows, D_MODEL + hp * LANES:D_MODEL + (hp + 1) * LANES] = both.astype(BF16)
                else:
                    for tl, tile in enumerate(tiles):
                        vt_ref[0, r * (PROJ_ROWS // TILE) + tl, sl, :] = tile


def _proj_kernel(x_ref, w_ref, b_ref, cos_ref, sp_ref, sm_ref, *rest, tm, dil, with_kmean, gqa):
    rest = list(rest)
    perm_ref = rest.pop(0) if dil > 1 else None
    qk_ref = rest.pop(0)
    vt_ref = rest.pop(0) if dil == 1 else None
    km_ref = rest.pop(0) if with_kmean else None
    sub = PROJ_ROWS // dil
    for r in range(tm // PROJ_ROWS):
        rows = slice(r * PROJ_ROWS, (r + 1) * PROJ_ROWS)
        xr = x_ref[0, rows, :].astype(BF16)
        if dil == 1:
            tabs = (cos_ref[0, rows, :], sp_ref[0, rows, :], sm_ref[0, rows, :])
        else:
            xr = jnp.dot(perm_ref[...], xr, preferred_element_type=F32).astype(BF16)
            tabs = tuple(
                jnp.concatenate([tb.at[0][pl.ds(r * PROJ_ROWS + ph, sub, stride=dil), :] for ph in range(dil)], axis=0)
                for tb in (cos_ref, sp_ref, sm_ref))
        for kind in range(3):
            if gqa and kind > 0:
                _proj_shared_kv(xr, w_ref, b_ref, tabs, qk_ref, vt_ref, kind, r)
                continue
            cols = slice(kind * D_MODEL, (kind + 1) * D_MODEL)
            acc = jnp.dot(xr, w_ref[:, cols], preferred_element_type=F32) + b_ref[:, cols]
            cs, sp, sm = [tb * Q_SCALE for tb in tabs] if kind == 0 else tabs
            for c in range(N_CHUNKS):
                sl = slice(c * LANES, (c + 1) * LANES)
                osl = slice(kind * D_MODEL + c * LANES, kind * D_MODEL + (c + 1) * LANES)
                y = acc[:, sl]
                if kind < 2:
                    y = _rotary(y, cs, sp, sm)
                if dil > 1:
                    yb = y.astype(BF16)
                    for ph in range(dil):
                        qk_ref[0, ph, r * sub:(r + 1) * sub, osl] = yb[ph * sub:(ph + 1) * sub]
                elif kind < 2:
                    qk_ref[0, 0, rows, osl] = y.astype(BF16)
                else:
                    for tl in range(PROJ_ROWS // TILE):
                        vt_ref[0, r * (PROJ_ROWS // TILE) + tl, sl, :] = y[tl * TILE:(tl + 1) * TILE].T.astype(BF16)
                if with_kmean and kind == 1:
                    km_ref[0, r, :, sl] = jnp.sum(y, axis=0, keepdims=True) * (1.0 / MOBA_BLOCK)


def _qkv_proj(xb, w, b, tables, *, dil=1, with_kmean=False, tm=512):
    bsz, seq, _ = xb.shape
    tm = min(tm, seq)
    gqa = w.shape[1] == D_MODEL + 2 * A_KV_HEADS * HEAD_DIM
    assert gqa or w.shape[1] == 3 * D_MODEL
    assert not gqa or (dil == 1 and not with_kmean)
    assert PROJ_ROWS == MOBA_BLOCK and tm % PROJ_ROWS == 0 and PROJ_ROWS % (16 * dil) == 0
    tabs = [tb.reshape(bsz, seq, LANES) for tb in tables]
    tab_spec = pl.BlockSpec((1, tm, LANES), lambda bi, i: (bi, i, 0))
    fix = lambda bi, i: (0, 0)
    once = dict(pipeline_mode=pl.Buffered(1))
    ncol = 2 * D_MODEL if dil == 1 else 3 * D_MODEL
    out_specs = [pl.BlockSpec((1, dil, tm // dil, ncol), lambda bi, i: (bi, 0, i, 0))]
    out_shape = [jax.ShapeDtypeStruct((bsz, dil, seq // dil, ncol), BF16)]
    if dil == 1:
        out_specs.append(pl.BlockSpec((1, tm // TILE, D_MODEL, TILE), lambda bi, i: (bi, i, 0, 0)))
        out_shape.append(jax.ShapeDtypeStruct((bsz, seq // TILE, D_MODEL, TILE), BF16))
    if with_kmean:
        out_specs.append(pl.BlockSpec((1, tm // MOBA_BLOCK, 1, D_MODEL), lambda bi, i: (bi, i, 0, 0)))
        out_shape.append(jax.ShapeDtypeStruct((bsz, seq // MOBA_BLOCK, 1, D_MODEL), F32))
    in_specs = [
        pl.BlockSpec((1, tm, D_MODEL), lambda bi, i: (bi, i, 0)),
        pl.BlockSpec((D_MODEL, w.shape[1]), fix, **once),
        pl.BlockSpec((1, w.shape[1]), fix),
        tab_spec, tab_spec, tab_spec,
    ]
    args = [xb, w, b, *tabs]
    if dil > 1:
        p = np.arange(PROJ_ROWS)
        perm = np.zeros((PROJ_ROWS, PROJ_ROWS), np.float32)
        perm[p, (p % (PROJ_ROWS // dil)) * dil + p // (PROJ_ROWS // dil)] = 1.0
        in_specs.append(pl.BlockSpec((PROJ_ROWS, PROJ_ROWS), fix))
        args.append(jnp.asarray(perm, BF16))
    return pl.pallas_call(
        functools.partial(_proj_kernel, tm=tm, dil=dil, with_kmean=with_kmean, gqa=gqa),
        grid=(bsz, seq // tm),
        in_specs=in_specs,
        out_specs=out_specs,
        out_shape=out_shape,
        compiler_params=_params("parallel", "parallel"),
        name="qkv_proj",
    )(*args)


def _band_bias(n_back):
    k = np.arange(TILE)[:, None]
    q = (np.arange(2 * TILE) % TILE)[None, :]
    prev = q + TILE - k <= n_back
    own = k <= q
    return jnp.asarray(np.where(np.stack([prev, np.zeros_like(prev), own]), 0.0, NEG), F32)


def _stack_heads(q):
    lane = lax.broadcasted_iota(jnp.int32, q.shape, 1)
    zero = jnp.zeros_like(q)
    return jnp.concatenate([jnp.where(lane < HEAD_DIM, q, zero), jnp.where(lane >= HEAD_DIM, q, zero)], axis=0)


BAND_QBLOCKS = 8


def _band_attn_kernel(*refs, with_sink, with_lse, v_transposed, nq, ns):
    refs = list(refs)
    sink_ref = refs.pop(0) if with_sink else None
    q_ref, kp_ref, kc_ref, vp_ref, vc_ref, bias_ref, o_ref = refs[:7]
    lse_ref = refs[7] if with_lse else None
    first = (pl.program_id(1) == 0).astype(jnp.int32)
    col = lax.broadcasted_iota(jnp.int32, (1, 2 * TILE), 1)
    for sq, t in [(sq, t) for sq in range(ns) for t in range(nq)]:
        rows = slice(t * TILE, (t + 1) * TILE)
        prow = slice((t - 1) * TILE, t * TILE)
        lse_rows = []
        for hp in range(HEAD_PAIRS):
            sl = slice(hp * LANES, (hp + 1) * LANES)
            qs = _stack_heads(q_ref[sq, rows, sl])
            if t == 0:
                k_prev, bias_prev = kp_ref[sq, :, sl], bias_ref[first]
                v_prev = vp_ref[sq, 0, sl, :] if v_transposed else vp_ref[sq, :, sl]
            else:
                k_prev, bias_prev = kc_ref[sq, prow, sl], bias_ref[0]
                v_prev = vc_ref[sq, t - 1, sl, :] if v_transposed else vc_ref[sq, prow, sl]
            v_own = vc_ref[sq, t, sl, :] if v_transposed else vc_ref[sq, rows, sl]
            parts = []
            for k, v, bias in ((k_prev, v_prev, bias_prev), (kc_ref[sq, rows, sl], v_own, bias_ref[2])):
                s = lax.dot_general(k, qs, _NT, preferred_element_type=F32) + bias
                m = jnp.max(s, axis=0, keepdims=True)
                p = jnp.exp2(s - m)
                l = jnp.sum(p, axis=0, keepdims=True)
                if v_transposed:
                    pv = jnp.dot(v, p.astype(BF16), preferred_element_type=F32)
                else:
                    pv = lax.dot_general(v, p.astype(BF16), _TN, preferred_element_type=F32)
                parts.append((m, l, pv))
            (m0, l0, pv0), (m1, l1, pv1) = parts
            m = jnp.maximum(m0, m1)
            if with_sink:
                sk = jnp.where(col < TILE, sink_ref[2 * hp], sink_ref[2 * hp + 1]) * LOG2E
                m = jnp.maximum(m, sk)
            a0, a1 = jnp.exp2(m0 - m), jnp.exp2(m1 - m)
            l = a0 * l0 + a1 * l1
            if with_sink:
                l = l + jnp.exp2(sk - m)
            inv = 1.0 / l
            w0, w1 = a0 * inv, a1 * inv
            o_t = jnp.concatenate(
                [pv0[:HEAD_DIM, :TILE] * w0[:, :TILE] + pv1[:HEAD_DIM, :TILE] * w1[:, :TILE],
                 pv0[HEAD_DIM:, TILE:] * w0[:, TILE:] + pv1[HEAD_DIM:, TILE:] * w1[:, TILE:]], axis=0)
            o_ref[sq, rows, sl] = o_t.T.astype(BF16)
            if with_lse:
                lse = (m + jnp.log2(l)) * LN2
                lse_rows += [lse[:, :TILE], lse[:, TILE:]]
        if with_lse:
            pad = jnp.zeros((LANES - N_HEADS, TILE), F32)
            lse_ref[sq, rows, :] = jnp.concatenate(lse_rows + [pad], axis=0).T


def _band_attention(qk, vt, *, n_back, sinks=None, with_lse=False):
    nseq, ln, _ = qk.shape
    nq = math.gcd(BAND_QBLOCKS, ln // TILE)
    ns = math.gcd(BAND_QBLOCKS // nq, nseq)
    blk = (ns, nq * TILE, D_MODEL)
    cur = lambda c: pl.BlockSpec(blk, lambda s, i: (s, i, c))
    prev = lambda c: pl.BlockSpec((ns, TILE, D_MODEL), lambda s, i: (s, jnp.maximum(i * nq - 1, 0), c))
    if vt is None:
        v_specs, v_args = [prev(2), cur(2)], [qk, qk]
    else:
        v_specs = [pl.BlockSpec((ns, 1, D_MODEL, TILE), lambda s, i: (s, jnp.maximum(i * nq - 1, 0), 0, 0)),
                   pl.BlockSpec((ns, nq, D_MODEL, TILE), lambda s, i: (s, i, 0, 0))]
        v_args = [vt, vt]
    in_specs = [cur(0), prev(1), cur(1)] + v_specs + [pl.BlockSpec((3, TILE, 2 * TILE), lambda s, i: (0, 0, 0))]
    args = [qk] * 3 + v_args + [_band_bias(n_back)]
    if sinks is not None:
        in_specs.insert(0, pl.BlockSpec(memory_space=pltpu.SMEM))
        args.insert(0, sinks)
    out_specs = [pl.BlockSpec(blk, lambda s, i: (s, i, 0))]
    out_shape = [jax.ShapeDtypeStruct((nseq, ln, D_MODEL), BF16)]
    if with_lse:
        out_specs.append(pl.BlockSpec((ns, nq * TILE, LANES), lambda s, i: (s, i, 0)))
        out_shape.append(jax.ShapeDtypeStruct((nseq, ln, LANES), F32))
    return pl.pallas_call(
        functools.partial(_band_attn_kernel, with_sink=sinks is not None, with_lse=with_lse,
                          v_transposed=vt is not None, nq=nq, ns=ns),
        grid=(nseq // ns, ln // (nq * TILE)),
        in_specs=in_specs,
        out_specs=out_specs,
        out_shape=out_shape,
        compiler_params=_params("parallel", "arbitrary"),
        name="band_attn",
    )(*args)


N_QSUB = MOBA_BLOCK // TILE
N_CHAINS = HEAD_PAIRS * N_QSUB
PAST_BLOCKS = 4


def _moba_kernel(q_ref, k_ref, vt_ref, km_ref, o_ref, qs_ref, sel_ref, m_ref, l_ref, acc_ref, *, nblk):
    qb = pl.program_id(1)
    blk = lax.broadcasted_iota(jnp.int32, (nblk, MOBA_BLOCK), 0)
    blkf = blk.astype(F32)
    lane = lax.broadcasted_iota(jnp.int32, (MOBA_BLOCK, LANES), 1)
    kidx = lax.broadcasted_iota(jnp.int32, (TILE, 2 * TILE), 0)
    qidx = lax.broadcasted_iota(jnp.int32, (TILE, 2 * TILE), 1) & (TILE - 1)
    causal = jnp.where(kidx <= qidx, 0.0, NEG).astype(F32)

    for hp in range(HEAD_PAIRS):
        sl = slice(hp * LANES, (hp + 1) * LANES)
        km = km_ref[0, :, sl]
        km_hi = km.astype(BF16)
        km_lo = (km - km_hi.astype(F32)).astype(BF16)
        for qs in range(N_QSUB):
            c = hp * N_QSUB + qs
            stacked = _stack_heads(q_ref[0, qs * TILE:(qs + 1) * TILE, sl])
            qs_ref[c] = stacked
            gate = (lax.dot_general(km_hi, stacked, _NT, preferred_element_type=F32)
                    + lax.dot_general(km_lo, stacked, _NT, preferred_element_type=F32))
            g = jnp.where(blk < qb, gate, NEG)
            sel = jnp.zeros((nblk, 2 * TILE), F32)
            for r in range(min(MOBA_TOPK, nblk)):
                mx = jnp.max(g, axis=0, keepdims=True)
                idx = jnp.min(jnp.where(g == mx, blkf, float(nblk)), axis=0, keepdims=True)
                hit = blkf == idx
                sel = jnp.where(hit & (r < qb), 1.0, sel)
                g = jnp.where(hit, -jnp.inf, g)
            sel_ref[c] = sel
    m_ref[...] = jnp.full(m_ref.shape, NEG, F32)
    l_ref[...] = jnp.zeros(l_ref.shape, F32)
    acc_ref[...] = jnp.zeros(acc_ref.shape, F32)

    def update(c, hp, jt, bias_row, bias_tile):
        sl = slice(hp * LANES, (hp + 1) * LANES)
        off = pl.multiple_of(jt * TILE, TILE)
        s = lax.dot_general(k_ref[0, pl.ds(off, TILE), sl], qs_ref[c], _NT, preferred_element_type=F32)
        if bias_tile is not None:
            s = s + bias_tile
        mj = jnp.max(s, axis=0, keepdims=True)
        m_old = m_ref[c]
        if bias_row is None:
            m_new = jnp.maximum(m_old, mj)
            p = jnp.exp2(s - m_new)
        else:
            m_new = jnp.maximum(m_old, jnp.where(bias_row, mj, NEG))
            p = jnp.exp2(s + jnp.where(bias_row, -m_new, NEG))
        alpha = jnp.exp2(m_old - m_new)
        m_ref[c] = m_new
        l_ref[c] = alpha * l_ref[c] + jnp.sum(p, axis=0, keepdims=True)
        pv = jnp.dot(vt_ref[0, jt, sl, :], p.astype(BF16), preferred_element_type=F32)
        acc_ref[c] = alpha * acc_ref[c] + jnp.concatenate([pv[:HEAD_DIM, :TILE], pv[HEAD_DIM:, TILE:]], axis=1)

    def past_blocks(j0, nb):
        for u in range(nb * N_QSUB):
            j = j0 + u // N_QSUB
            for c in range(N_CHAINS):
                update(c, c // N_QSUB, j0 * N_QSUB + u, sel_ref[c, pl.ds(j, 1), :] > 0.5, None)

    def past(jj, carry):
        past_blocks(jj * PAST_BLOCKS, PAST_BLOCKS)
        return carry

    lax.fori_loop(0, qb // PAST_BLOCKS, past, 0)
    rem = qb % PAST_BLOCKS
    nb = PAST_BLOCKS // 2
    while nb >= 1:
        @pl.when(rem & nb != 0)
        def _(nb=nb):
            past_blocks(qb - (rem & (2 * nb - 1)), nb)
        nb //= 2

    for c in range(N_CHAINS):
        hp, qs = c // N_QSUB, c % N_QSUB
        for kt in range(qs + 1):
            update(c, hp, qb * N_QSUB + kt, None, causal if kt == qs else None)

    for c in range(N_CHAINS):
        hp, qs = c // N_QSUB, c % N_QSUB
        o = acc_ref[c] * (1.0 / l_ref[c])
        o_t = jnp.concatenate([o[:, :TILE], o[:, TILE:]], axis=0)
        o_ref[0, qs * TILE:(qs + 1) * TILE, hp * LANES:(hp + 1) * LANES] = o_t.T.astype(BF16)


def _moba_attention(qk, vt, kmean):
    bsz, seq, _ = qk.shape
    nblk = seq // MOBA_BLOCK
    return pl.pallas_call(
        functools.partial(_moba_kernel, nblk=nblk),
        grid=(bsz, nblk),
        in_specs=[
            pl.BlockSpec((1, MOBA_BLOCK, D_MODEL), lambda b, i: (b, i, 0)),
            pl.BlockSpec((1, seq, D_MODEL), lambda b, i: (b, 0, 1)),
            pl.BlockSpec((1, seq // TILE, D_MODEL, TILE), lambda b, i: (b, 0, 0, 0)),
            pl.BlockSpec((1, nblk, D_MODEL), lambda b, i: (b, 0, 0)),
        ],
        out_specs=pl.BlockSpec((1, MOBA_BLOCK, D_MODEL), lambda b, i: (b, i, 0)),
        out_shape=jax.ShapeDtypeStruct((bsz, seq, D_MODEL), BF16),
        scratch_shapes=[pltpu.VMEM((N_CHAINS, 2 * TILE, LANES), BF16),
                        pltpu.VMEM((N_CHAINS, nblk, 2 * TILE), F32),
                        pltpu.VMEM((N_CHAINS, 1, 2 * TILE), F32),
                        pltpu.VMEM((N_CHAINS, 1, 2 * TILE), F32),
                        pltpu.VMEM((N_CHAINS, HEAD_DIM, 2 * TILE), F32)],
        compiler_params=_params("parallel", "arbitrary"),
        name="moba_attn",
    )(qk, qk, vt, kmean)


def _layer_norm(y, g, b):
    mu = jnp.mean(y, axis=-1, keepdims=True)
    d = y - mu
    var = jnp.mean(d * d, axis=-1, keepdims=True)
    return d * lax.rsqrt(var + LN_EPS) * g + b


def _oproj_ln_kernel(*refs, dils, tm):
    n = len(dils)
    o_refs, refs = refs[:n], refs[n:]
    if n > 1:
        lse_refs, e_ref, refs = refs[:n], refs[n], refs[n + 1:]
    w_ref, b_ref, x_ref, g_ref, beta_ref, out_ref, outb_ref = refs[:7]
    scr = refs[7:]
    if n == 1:
        o = o_refs[0][0, 0]
    else:
        o_scr, lse_scr = scr
        lses = []
        for gi, d in enumerate(dils):
            if d == 1:
                lses.append(lse_refs[gi][0, 0])
            else:
                for ph in range(d):
                    lse_scr.at[gi][pl.ds(ph, tm // d, stride=d), :] = lse_refs[gi][0, ph]
                lses.append(lse_scr[gi])
        mx = functools.reduce(jnp.maximum, lses)
        es = [jnp.exp(l - mx) for l in lses]
        inv = 1.0 / functools.reduce(lambda a, c: a + c, es)
        chunks = [None] * N_CHUNKS
        for gi, d in enumerate(dils):
            wgt = es[gi] * inv
            hi = wgt.astype(BF16)
            lo = (wgt - hi.astype(F32)).astype(BF16)
            wide = (jnp.dot(hi, e_ref[...], preferred_element_type=F32)
                    + jnp.dot(lo, e_ref[...], preferred_element_type=F32))
            for c in range(N_CHUNKS):
                sl = slice(c * LANES, (c + 1) * LANES)
                if d == 1:
                    og = o_refs[gi][0, 0, :, sl].astype(F32)
                else:
                    for ph in range(d):
                        o_scr.at[gi, c][pl.ds(ph, tm // d, stride=d), :] = o_refs[gi][0, ph, :, sl].astype(F32)
                    og = o_scr[gi, c]
                term = wide[:, sl] * og
                chunks[c] = term if chunks[c] is None else chunks[c] + term
        o = jnp.concatenate(chunks, axis=1).astype(BF16)
    h = jnp.dot(o, w_ref[...], preferred_element_type=F32) + b_ref[...]
    y = _layer_norm(DN_ALPHA * x_ref[0] + h, g_ref[...], beta_ref[...])
    out_ref[0] = y
    outb_ref[0] = y.astype(BF16)


def _oproj_ln(os_, lses, dils, w, b, x, g, beta, *, tm=512):
    bsz, seq, _ = x.shape
    tm = min(tm, seq)
    n = len(os_)
    tok = lambda bi, i: (bi, i, 0)
    ph = lambda bi, i: (bi, 0, i, 0)
    fix = lambda bi, i: (0, 0)
    in_specs = [pl.BlockSpec((1, d, tm // d, D_MODEL), ph) for d in dils]
    args = list(os_)
    scratch = []
    if n > 1:
        in_specs += [pl.BlockSpec((1, d, tm // d, LANES), ph) for d in dils]
        args += list(lses)
        expand = np.zeros((LANES, D_MODEL), np.float32)
        for hd in range(N_HEADS):
            expand[hd, hd * HEAD_DIM:(hd + 1) * HEAD_DIM] = 1.0
        in_specs.append(pl.BlockSpec((LANES, D_MODEL), fix))
        args.append(jnp.asarray(expand, BF16))
        scratch = [pltpu.VMEM((n, N_CHUNKS, tm, LANES), F32), pltpu.VMEM((n, tm, LANES), F32)]
    in_specs += [pl.BlockSpec((D_MODEL, D_MODEL), fix), pl.BlockSpec((1, D_MODEL), fix),
                 pl.BlockSpec((1, tm, D_MODEL), tok), pl.BlockSpec((1, D_MODEL), fix), pl.BlockSpec((1, D_MODEL), fix)]
    args += [w, b, x, g, beta]
    return pl.pallas_call(
        functools.partial(_oproj_ln_kernel, dils=tuple(dils), tm=tm),
        grid=(bsz, seq // tm),
        in_specs=in_specs,
        out_specs=[pl.BlockSpec((1, tm, D_MODEL), tok), pl.BlockSpec((1, tm, D_MODEL), tok)],
        out_shape=[jax.ShapeDtypeStruct((bsz, seq, D_MODEL), F32), jax.ShapeDtypeStruct((bsz, seq, D_MODEL), BF16)],
        scratch_shapes=scratch,
        compiler_params=_params("parallel", "parallel"),
        name="oproj_ln",
    )(*args)


FFN_ROWS = 256


def _swiglu_ln(x, xb, wg_ref, wu_ref, wd_ref, g_ref, beta_ref):
    gate = jnp.dot(xb, wg_ref[...], preferred_element_type=F32)
    up = jnp.dot(xb, wu_ref[...], preferred_element_type=F32)
    hid = (gate * jax.nn.sigmoid(gate) * up).astype(BF16)
    y = jnp.dot(hid, wd_ref[...], preferred_element_type=F32)
    return _layer_norm(DN_ALPHA * x + y, g_ref[...], beta_ref[...])


def _ffn_ln_kernel(x_ref, xb_ref, wg_ref, wu_ref, wd_ref, g_ref, beta_ref, out_ref, outb_ref):
    for r in range(x_ref.shape[0] // FFN_ROWS):
        rows = slice(r * FFN_ROWS, (r + 1) * FFN_ROWS)
        y = _swiglu_ln(x_ref[rows, :], xb_ref[rows, :], wg_ref, wu_ref, wd_ref, g_ref, beta_ref)
        out_ref[rows, :] = y
        outb_ref[rows, :] = y.astype(BF16)


def _oproj_ffn_kernel(o_ref, wo_ref, bo_ref, x_ref, g0_ref, beta0_ref, wg_ref, wu_ref, wd_ref, g1_ref, beta1_ref,
                      out_ref, outb_ref):
    for r in range(x_ref.shape[0] // FFN_ROWS):
        rows = slice(r * FFN_ROWS, (r + 1) * FFN_ROWS)
        h = jnp.dot(o_ref[rows, :], wo_ref[...], preferred_element_type=F32) + bo_ref[...]
        x1 = _layer_norm(DN_ALPHA * x_ref[rows, :] + h, g0_ref[...], beta0_ref[...])
        y = _swiglu_ln(x1, x1.astype(BF16), wg_ref, wu_ref, wd_ref, g1_ref, beta1_ref)
        out_ref[rows, :] = y
        outb_ref[rows, :] = y.astype(BF16)


def _oproj_ffn(o, wo, bo, x, g0, beta0, wg, wu, wd, g1, beta1, *, tm=512):
    t = x.shape[0]
    tm = min(tm, t)
    row = lambda i: (i, 0)
    fix = lambda i: (0, 0)
    resident = dict(pipeline_mode=pl.Buffered(1))
    vec = pl.BlockSpec((1, D_MODEL), fix)
    return pl.pallas_call(
        _oproj_ffn_kernel,
        grid=(t // tm,),
        in_specs=[pl.BlockSpec((tm, D_MODEL), row), pl.BlockSpec((D_MODEL, D_MODEL), fix, **resident), vec,
                  pl.BlockSpec((tm, D_MODEL), row), vec, vec,
                  pl.BlockSpec((D_MODEL, D_FF), fix, **resident),
                  pl.BlockSpec((D_MODEL, D_FF), fix, **resident),
                  pl.BlockSpec((D_FF, D_MODEL), fix, **resident), vec, vec],
        out_specs=[pl.BlockSpec((tm, D_MODEL), row), pl.BlockSpec((tm, D_MODEL), row)],
        out_shape=[jax.ShapeDtypeStruct((t, D_MODEL), F32), jax.ShapeDtypeStruct((t, D_MODEL), BF16)],
        compiler_params=_params("parallel"),
        name="oproj_ffn",
    )(o, wo, bo, x, g0, beta0, wg, wu, wd, g1, beta1)


def _ffn_ln(x, xb, wg, wu, wd, g, beta, *, tm=1024):
    t = x.shape[0]
    tm = min(tm, t)
    row = lambda i: (i, 0)
    fix = lambda i: (0, 0)
    resident = dict(pipeline_mode=pl.Buffered(1))
    return pl.pallas_call(
        _ffn_ln_kernel,
        grid=(t // tm,),
        in_specs=[pl.BlockSpec((tm, D_MODEL), row), pl.BlockSpec((tm, D_MODEL), row),
                  pl.BlockSpec((D_MODEL, D_FF), fix, **resident),
                  pl.BlockSpec((D_MODEL, D_FF), fix, **resident),
                  pl.BlockSpec((D_FF, D_MODEL), fix, **resident),
                  pl.BlockSpec((1, D_MODEL), fix), pl.BlockSpec((1, D_MODEL), fix)],
        out_specs=[pl.BlockSpec((tm, D_MODEL), row), pl.BlockSpec((tm, D_MODEL), row)],
        out_shape=[jax.ShapeDtypeStruct((t, D_MODEL), F32), jax.ShapeDtypeStruct((t, D_MODEL), BF16)],
        compiler_params=_params("parallel"),
        name="ffn_ln",
    )(x, xb, wg, wu, wd, g, beta)


def kernel(x, positions, ln_g, ln_b, a_w_qkv, a_b_qkv, a_sinks, a_w_o, a_b_o, b_w_qkv, b_w_o, c_w_qkv, c_w_o,
           w_gate_up, w_down):
    bsz, seq, d = x.shape
    assert d == D_MODEL and seq % (B_GROUPS[-1][1] * TILE) == 0 and seq % MOBA_BLOCK == 0
    t = bsz * seq
    tables = _rope_tables(positions.reshape(t, 1))
    xf = xb = x
    zero_bias = jnp.zeros((1, D_MODEL), F32)
    zero_bias3 = jnp.zeros((1, 3 * D_MODEL), F32)
    row = lambda v: v.reshape(1, -1)
    for i in range(DEPTH):
        kind, j = i % 3, i // 3
        g0, b0, g1, b1 = row(ln_g[i, 0]), row(ln_b[i, 0]), row(ln_g[i, 1]), row(ln_b[i, 1])
        wgu = w_gate_up[i]
        ffn_w = (wgu[:, :D_FF].astype(BF16), wgu[:, D_FF:].astype(BF16), w_down[i].astype(BF16))
        if kind == 1:
            w = b_w_qkv[j].astype(BF16)
            os_, lses, dils = [], [], []
            for gi, (window, dil) in enumerate(B_GROUPS):
                cols = slice(3 * gi * D_MODEL, 3 * (gi + 1) * D_MODEL)
                ln = seq // dil
                if dil == 1:
                    qk, vt = _qkv_proj(xb, w[:, cols], zero_bias3, tables)
                    qk = qk[:, 0]
                else:
                    qk, vt = _qkv_proj(xb, w[:, cols], zero_bias3, tables, dil=dil)[0], None
                    qk = qk.reshape(bsz * dil, ln, 3 * D_MODEL)
                o, lse = _band_attention(qk, vt, n_back=window // dil, with_lse=True)
                os_.append(o.reshape(bsz, dil, ln, D_MODEL))
                lses.append(lse.reshape(bsz, dil, ln, LANES))
                dils.append(dil)
            xf, xb = _oproj_ln(os_, lses, dils, b_w_o[j].astype(BF16), zero_bias, xf, g0, b0)
            y, yb = _ffn_ln(xf.reshape(t, d), xb.reshape(t, d), *ffn_w, g1, b1)
        else:
            if kind == 0:
                qk, vt = _qkv_proj(xb, a_w_qkv[j].astype(BF16), row(a_b_qkv[j]), tables)
                o = _band_attention(qk[:, 0], vt, n_back=A_WINDOW - 1, sinks=a_sinks[j])[0]
                wo, bo = a_w_o[j], row(a_b_o[j])
            else:
                qk, vt, kmean = _qkv_proj(xb, c_w_qkv[j].astype(BF16), zero_bias3, tables, with_kmean=True)
                o = _moba_attention(qk[:, 0], vt, kmean[:, :, 0])
                wo, bo = c_w_o[j], zero_bias
            y, yb = _oproj_ffn(o.reshape(t, d), wo.astype(BF16), bo, xf.reshape(t, d), g0, b0, *ffn_w, g1, b1)
        xf, xb = y.reshape(bsz, seq, d), yb.reshape(bsz, seq, d)
    return xf
```

```python
import functools
import math

import numpy as np
import jax
import jax.numpy as jnp
from jax import lax
from jax.experimental import pallas as pl
from jax.experimental.pallas import tpu as pltpu

D_MODEL = 1024
N_HEADS = 16
HEAD_DIM = 64
ROT_DIM = HEAD_DIM // 4
ROPE_THETA = 500000.0
A_KV_HEADS = 4
A_WINDOW = 128
B_GROUPS = ((128, 1), (512, 4), (2048, 16))
MOBA_BLOCK = 256
MOBA_TOPK = 3
D_FF = 2816
DEPTH = 4
DN_ALPHA = (2 * DEPTH) ** 0.25
LN_EPS = 1e-5
NEG = -1e30

LANES = 128
TILE = 128
HEAD_PAIRS = N_HEADS // 2
N_CHUNKS = D_MODEL // LANES
VMEM_LIMIT = 56 * 1024 * 1024
LOG2E = math.log2(math.e)
LN2 = math.log(2.0)
Q_SCALE = HEAD_DIM ** -0.5 * LOG2E

F32 = jnp.float32
BF16 = jnp.bfloat16
_NT = (((1,), (1,)), ((), ()))
_TN = (((0,), (0,)), ((), ()))


def _params(*sem):
    return pltpu.CompilerParams(dimension_semantics=sem, vmem_limit_bytes=VMEM_LIMIT)


_INV_FREQ = [float(np.float32(ROPE_THETA ** (-(2.0 * i) / ROT_DIM))) for i in range(ROT_DIM // 2)]


def _rope_table_kernel(pos_ref, cos_ref, sp_ref, sm_ref):
    pos = pos_ref[...].astype(F32)
    lane = lax.broadcasted_iota(jnp.int32, (1, LANES), 1)
    c = lane & (HEAD_DIM - 1)
    f = c & (ROT_DIM // 2 - 1)
    inv = jnp.zeros((1, LANES), F32)
    for i, v in enumerate(_INV_FREQ):
        inv = jnp.where(f == i, v, inv)
    ang = pos * inv
    cs, sn = jnp.cos(ang), jnp.sin(ang)
    cos_ref[...] = jnp.where(c < ROT_DIM, cs, 1.0)
    sp_ref[...] = jnp.where((c >= ROT_DIM // 2) & (c < ROT_DIM), sn, 0.0)
    sm_ref[...] = jnp.where(c < ROT_DIM // 2, -sn, 0.0)


def _rope_tables(pos_col):
    t = pos_col.shape[0]
    tm = min(t, 2048)
    spec = pl.BlockSpec((tm, LANES), lambda i: (i, 0))
    shp = jax.ShapeDtypeStruct((t, LANES), F32)
    return pl.pallas_call(
        _rope_table_kernel,
        grid=(t // tm,),
        in_specs=[pl.BlockSpec((tm, 1), lambda i: (i, 0))],
        out_specs=[spec, spec, spec],
        out_shape=[shp, shp, shp],
        compiler_params=_params("parallel"),
        name="rope_tables",
    )(pos_col)


PROJ_ROWS = 256


def _rotary(y, cs, sp, sm):
    return y * cs + pltpu.roll(y, ROT_DIM // 2, 1) * sp + pltpu.roll(y, LANES - ROT_DIM // 2, 1) * sm


def _proj_shared_kv(xr, w_ref, b_ref, tabs, qk_ref, vt_ref, kind, r):
    kv = A_KV_HEADS * HEAD_DIM
    base = D_MODEL + (kind - 1) * kv
    acc = jnp.dot(xr, w_ref[:, base:base + kv], preferred_element_type=F32) + b_ref[:, base:base + kv]
    lane = lax.broadcasted_iota(jnp.int32, (PROJ_ROWS, LANES), 1)
    rows = slice(r * PROJ_ROWS, (r + 1) * PROJ_ROWS)
    pairs_per_kv = N_HEADS // A_KV_HEADS // 2
    for cc in range(kv // LANES):
        y = acc[:, cc * LANES:(cc + 1) * LANES]
        if kind == 1:
            y = _rotary(y, *tabs)
        swapped = pltpu.roll(y, HEAD_DIM, 1)
        for par, both in enumerate((jnp.where(lane < HEAD_DIM, y, swapped), jnp.where(lane < HEAD_DIM, swapped, y))):
            g = 2 * cc + par
            if kind == 2:
                tiles = [both[tl * TILE:(tl + 1) * TILE].T.astype(BF16) for tl in range(PROJ_ROWS // TILE)]
            for hp in range(g * pairs_per_kv, (g + 1) * pairs_per_kv):
                sl = slice(hp * LANES, (hp + 1) * LANES)
                if kind == 1:
                    qk_ref[0, 0, rows, D_MODEL + hp * LANES:D_MODEL + (hp + 1) * LANES] = both.astype(BF16)
                else:
                    for tl, tile in enumerate(tiles):
                        vt_ref[0, r * (PROJ_ROWS // TILE) + tl, sl, :] = tile


def _proj_kernel(x_ref, w_ref, b_ref, cos_ref, sp_ref, sm_ref, *rest, tm, dil, with_kmean, gqa):
    rest = list(rest)
    perm_ref = rest.pop(0) if dil > 1 else None
    qk_ref = rest.pop(0)
    vt_ref = rest.pop(0) if dil == 1 else None
    km_ref = rest.pop(0) if with_kmean else None
    sub = PROJ_ROWS // dil
    for r in range(tm // PROJ_ROWS):
        rows = slice(r * PROJ_ROWS, (r + 1) * PROJ_ROWS)
        xr = x_ref[0, rows, :].astype(BF16)
        if dil == 1:
            tabs = (cos_ref[0, rows, :], sp_ref[0, rows, :], sm_ref[0, rows, :])
        else:
            xr = jnp.dot(perm_ref[...], xr, preferred_element_type=F32).astype(BF16)
            tabs = tuple(
                jnp.concatenate([tb.at[0][pl.ds(r * PROJ_ROWS + ph, sub, stride=dil), :] for ph in range(dil)], axis=0)
                for tb in (cos_ref, sp_ref, sm_ref))
        for kind in range(3):
            if gqa and kind > 0:
                _proj_shared_kv(xr, w_ref, b_ref, tabs, qk_ref, vt_ref, kind, r)
                continue
            cols = slice(kind * D_MODEL, (kind + 1) * D_MODEL)
            acc = jnp.dot(xr, w_ref[:, cols], preferred_element_type=F32) + b_ref[:, cols]
            cs, sp, sm = [tb * Q_SCALE for tb in tabs] if kind == 0 else tabs
            for c in range(N_CHUNKS):
                sl = slice(c * LANES, (c + 1) * LANES)
                osl = slice(kind * D_MODEL + c * LANES, kind * D_MODEL + (c + 1) * LANES)
                y = acc[:, sl]
                if kind < 2:
                    y = _rotary(y, cs, sp, sm)
                if dil > 1:
                    yb = y.astype(BF16)
                    for ph in range(dil):
                        qk_ref[0, ph, r * sub:(r + 1) * sub, osl] = yb[ph * sub:(ph + 1) * sub]
                elif kind < 2:
                    qk_ref[0, 0, rows, osl] = y.astype(BF16)
                else:
                    for tl in range(PROJ_ROWS // TILE):
                        vt_ref[0, r * (PROJ_ROWS // TILE) + tl, sl, :] = y[tl * TILE:(tl + 1) * TILE].T.astype(BF16)
                if with_kmean and kind == 1:
                    km_ref[0, r, :, sl] = jnp.sum(y, axis=0, keepdims=True) * (1.0 / MOBA_BLOCK)


def _qkv_proj(xb, w, b, tables, *, dil=1, with_kmean=False, tm=512):
    bsz, seq, _ = xb.shape
    tm = min(tm, seq)
    gqa = w.shape[1] == D_MODEL + 2 * A_KV_HEADS * HEAD_DIM
    assert gqa or w.shape[1] == 3 * D_MODEL
    assert not gqa or (dil == 1 and not with_kmean)
    assert PROJ_ROWS == MOBA_BLOCK and tm % PROJ_ROWS == 0 and PROJ_ROWS % (16 * dil) == 0
    tabs = [tb.reshape(bsz, seq, LANES) for tb in tables]
    tab_spec = pl.BlockSpec((1, tm, LANES), lambda bi, i: (bi, i, 0))
    fix = lambda bi, i: (0, 0)
    once = dict(pipeline_mode=pl.Buffered(1))
    ncol = 2 * D_MODEL if dil == 1 else 3 * D_MODEL
    out_specs = [pl.BlockSpec((1, dil, tm // dil, ncol), lambda bi, i: (bi, 0, i, 0))]
    out_shape = [jax.ShapeDtypeStruct((bsz, dil, seq // dil, ncol), BF16)]
    if dil == 1:
        out_specs.append(pl.BlockSpec((1, tm // TILE, D_MODEL, TILE), lambda bi, i: (bi, i, 0, 0)))
        out_shape.append(jax.ShapeDtypeStruct((bsz, seq // TILE, D_MODEL, TILE), BF16))
    if with_kmean:
        out_specs.append(pl.BlockSpec((1, tm // MOBA_BLOCK, 1, D_MODEL), lambda bi, i: (bi, i, 0, 0)))
        out_shape.append(jax.ShapeDtypeStruct((bsz, seq // MOBA_BLOCK, 1, D_MODEL), F32))
    in_specs = [
        pl.BlockSpec((1, tm, D_MODEL), lambda bi, i: (bi, i, 0)),
        pl.BlockSpec((D_MODEL, w.shape[1]), fix, **once),
        pl.BlockSpec((1, w.shape[1]), fix),
        tab_spec, tab_spec, tab_spec,
    ]
    args = [xb, w, b, *tabs]
    if dil > 1:
        p = np.arange(PROJ_ROWS)
        perm = np.zeros((PROJ_ROWS, PROJ_ROWS), np.float32)
        perm[p, (p % (PROJ_ROWS // dil)) * dil + p // (PROJ_ROWS // dil)] = 1.0
        in_specs.append(pl.BlockSpec((PROJ_ROWS, PROJ_ROWS), fix))
        args.append(jnp.asarray(perm, BF16))
    return pl.pallas_call(
        functools.partial(_proj_kernel, tm=tm, dil=dil, with_kmean=with_kmean, gqa=gqa),
        grid=(bsz, seq // tm),
        in_specs=in_specs,
        out_specs=out_specs,
        out_shape=out_shape,
        compiler_params=_params("parallel", "parallel"),
        name="qkv_proj",
    )(*args)


def _band_bias(n_back):
    k = np.arange(TILE)[:, None]
    q = (np.arange(2 * TILE) % TILE)[None, :]
    prev = q + TILE - k <= n_back
    own = k <= q
    return jnp.asarray(np.where(np.stack([prev, np.zeros_like(prev), own]), 0.0, NEG), F32)


def _stack_heads(q):
    lane = lax.broadcasted_iota(jnp.int32, q.shape, 1)
    zero = jnp.zeros_like(q)
    return jnp.concatenate([jnp.where(lane < HEAD_DIM, q, zero), jnp.where(lane >= HEAD_DIM, q, zero)], axis=0)


BAND_QBLOCKS = 8


def _band_attn_kernel(*refs, with_sink, with_lse, v_transposed, nq, ns, merged):
    refs = list(refs)
    sink_ref = refs.pop(0) if with_sink else None
    q_ref, kp_ref, kc_ref, vp_ref, vc_ref, bias_ref, o_ref = refs[:7]
    lse_ref = refs[7] if with_lse else None
    first = (pl.program_id(1) == 0).astype(jnp.int32)
    col = lax.broadcasted_iota(jnp.int32, (1, 2 * TILE), 1)
    own_f32 = (lax.broadcasted_iota(jnp.int32, (TILE, 2 * TILE), 0)
               <= (lax.broadcasted_iota(jnp.int32, (TILE, 2 * TILE), 1) & (TILE - 1)))
    for sq, t in [(sq, t) for sq in range(ns) for t in range(nq)]:
        rows = slice(t * TILE, (t + 1) * TILE)
        prow = slice((t - 1) * TILE, t * TILE)
        lse_rows = []
        for hp in range(HEAD_PAIRS):
            sl = slice(hp * LANES, (hp + 1) * LANES)
            qs = _stack_heads(q_ref[sq, rows, sl])
            if t == 0:
                k_prev, bias_prev = kp_ref[sq, :, sl], bias_ref[first]
                v_prev = vp_ref[sq, 0, sl, :] if v_transposed else vp_ref[sq, :, sl]
            else:
                k_prev, bias_prev = kc_ref[sq, prow, sl], bias_ref[0]
                v_prev = vc_ref[sq, t - 1, sl, :] if v_transposed else vc_ref[sq, prow, sl]
            v_own = vc_ref[sq, t, sl, :] if v_transposed else vc_ref[sq, rows, sl]
            if merged:
                s_prev = lax.dot_general(k_prev, qs, _NT, preferred_element_type=F32)
                if t == 0:
                    s_prev = s_prev + jnp.where(first == 1, NEG, 0.0)
                s_own = lax.dot_general(kc_ref[sq, rows, sl], qs, _NT, preferred_element_type=F32)
                s = jnp.where(own_f32, s_own, s_prev)
                m = jnp.max(s, axis=0, keepdims=True)
                if with_sink:
                    sk = jnp.where(col < TILE, sink_ref[2 * hp], sink_ref[2 * hp + 1]) * LOG2E
                    m = jnp.maximum(m, sk)
                p = jnp.exp2(s - m)
                l = jnp.sum(p, axis=0, keepdims=True)
                if with_sink:
                    l = l + jnp.exp2(sk - m)
                p_own = jnp.where(own_f32, p, 0.0).astype(BF16)
                p_prev = jnp.where(own_f32, 0.0, p).astype(BF16)
                if v_transposed:
                    pv = (jnp.dot(v_own, p_own, preferred_element_type=F32)
                          + jnp.dot(v_prev, p_prev, preferred_element_type=F32))
                else:
                    pv = (lax.dot_general(v_own, p_own, _TN, preferred_element_type=F32)
                          + lax.dot_general(v_prev, p_prev, _TN, preferred_element_type=F32))
                inv = 1.0 / l
                o_t = jnp.concatenate([pv[:HEAD_DIM, :TILE] * inv[:, :TILE], pv[HEAD_DIM:, TILE:] * inv[:, TILE:]],
                                      axis=0)
                o_ref[sq, rows, sl] = o_t.T.astype(BF16)
                if with_lse:
                    lse = (m + jnp.log2(l)) * LN2
                    lse_rows += [lse[:, :TILE], lse[:, TILE:]]
                continue
            parts = []
            for k, v, bias in ((k_prev, v_prev, bias_prev), (kc_ref[sq, rows, sl], v_own, bias_ref[2])):
                s = lax.dot_general(k, qs, _NT, preferred_element_type=F32) + bias
                m = jnp.max(s, axis=0, keepdims=True)
                p = jnp.exp2(s - m)
                l = jnp.sum(p, axis=0, keepdims=True)
                if v_transposed:
                    pv = jnp.dot(v, p.astype(BF16), preferred_element_type=F32)
                else:
                    pv = lax.dot_general(v, p.astype(BF16), _TN, preferred_element_type=F32)
                parts.append((m, l, pv))
            (m0, l0, pv0), (m1, l1, pv1) = parts
            m = jnp.maximum(m0, m1)
            if with_sink:
                sk = jnp.where(col < TILE, sink_ref[2 * hp], sink_ref[2 * hp + 1]) * LOG2E
                m = jnp.maximum(m, sk)
            a0, a1 = jnp.exp2(m0 - m), jnp.exp2(m1 - m)
            l = a0 * l0 + a1 * l1
            if with_sink:
                l = l + jnp.exp2(sk - m)
            inv = 1.0 / l
            w0, w1 = a0 * inv, a1 * inv
            o_t = jnp.concatenate(
                [pv0[:HEAD_DIM, :TILE] * w0[:, :TILE] + pv1[:HEAD_DIM, :TILE] * w1[:, :TILE],
                 pv0[HEAD_DIM:, TILE:] * w0[:, TILE:] + pv1[HEAD_DIM:, TILE:] * w1[:, TILE:]], axis=0)
            o_ref[sq, rows, sl] = o_t.T.astype(BF16)
            if with_lse:
                lse = (m + jnp.log2(l)) * LN2
                lse_rows += [lse[:, :TILE], lse[:, TILE:]]
        if with_lse:
            pad = jnp.zeros((LANES - N_HEADS, TILE), F32)
            lse_ref[sq, rows, :] = jnp.concatenate(lse_rows + [pad], axis=0).T


def _band_attention(qk, vt, *, n_back, sinks=None, with_lse=False):
    nseq, ln, _ = qk.shape
    nq = math.gcd(BAND_QBLOCKS, ln // TILE)
    ns = math.gcd(BAND_QBLOCKS // nq, nseq)
    blk = (ns, nq * TILE, D_MODEL)
    cur = lambda c: pl.BlockSpec(blk, lambda s, i: (s, i, c))
    prev = lambda c: pl.BlockSpec((ns, TILE, D_MODEL), lambda s, i: (s, jnp.maximum(i * nq - 1, 0), c))
    if vt is None:
        v_specs, v_args = [prev(2), cur(2)], [qk, qk]
    else:
        v_specs = [pl.BlockSpec((ns, 1, D_MODEL, TILE), lambda s, i: (s, jnp.maximum(i * nq - 1, 0), 0, 0)),
                   pl.BlockSpec((ns, nq, D_MODEL, TILE), lambda s, i: (s, i, 0, 0))]
        v_args = [vt, vt]
    in_specs = [cur(0), prev(1), cur(1)] + v_specs + [pl.BlockSpec((3, TILE, 2 * TILE), lambda s, i: (0, 0, 0))]
    args = [qk] * 3 + v_args + [_band_bias(n_back)]
    if sinks is not None:
        in_specs.insert(0, pl.BlockSpec(memory_space=pltpu.SMEM))
        args.insert(0, sinks)
    out_specs = [pl.BlockSpec(blk, lambda s, i: (s, i, 0))]
    out_shape = [jax.ShapeDtypeStruct((nseq, ln, D_MODEL), BF16)]
    if with_lse:
        out_specs.append(pl.BlockSpec((ns, nq * TILE, LANES), lambda s, i: (s, i, 0)))
        out_shape.append(jax.ShapeDtypeStruct((nseq, ln, LANES), F32))
    return pl.pallas_call(
        functools.partial(_band_attn_kernel, with_sink=sinks is not None, with_lse=with_lse,
                          v_transposed=vt is not None, nq=nq, ns=ns, merged=n_back == TILE - 1),
        grid=(nseq // ns, ln // (nq * TILE)),
        in_specs=in_specs,
        out_specs=out_specs,
        out_shape=out_shape,
        compiler_params=_params("parallel", "arbitrary"),
        name="band_attn",
    )(*args)


N_QSUB = MOBA_BLOCK // TILE
N_CHAINS = HEAD_PAIRS * N_QSUB
PAST_BLOCKS = 4


def _moba_kernel(q_ref, k_ref, vt_ref, km_ref, o_ref, qs_ref, sel_ref, m_ref, l_ref, acc_ref, *, nblk):
    qb = pl.program_id(1)
    blk = lax.broadcasted_iota(jnp.int32, (nblk, MOBA_BLOCK), 0)
    blkf = blk.astype(F32)
    lane = lax.broadcasted_iota(jnp.int32, (MOBA_BLOCK, LANES), 1)
    kidx = lax.broadcasted_iota(jnp.int32, (TILE, 2 * TILE), 0)
    qidx = lax.broadcasted_iota(jnp.int32, (TILE, 2 * TILE), 1) & (TILE - 1)
    causal = jnp.where(kidx <= qidx, 0.0, NEG).astype(F32)

    for hp in range(HEAD_PAIRS):
        sl = slice(hp * LANES, (hp + 1) * LANES)
        km = km_ref[0, :, sl]
        km_hi = km.astype(BF16)
        km_lo = (km - km_hi.astype(F32)).astype(BF16)
        for qs in range(N_QSUB):
            c = hp * N_QSUB + qs
            stacked = _stack_heads(q_ref[0, qs * TILE:(qs + 1) * TILE, sl])
            qs_ref[c] = stacked
            gate = (lax.dot_general(km_hi, stacked, _NT, preferred_element_type=F32)
                    + lax.dot_general(km_lo, stacked, _NT, preferred_element_type=F32))
            g = jnp.where(blk < qb, gate, NEG)
            sel = jnp.zeros((nblk, 2 * TILE), F32)
            for r in range(min(MOBA_TOPK, nblk)):
                mx = jnp.max(g, axis=0, keepdims=True)
                idx = jnp.min(jnp.where(g == mx, blkf, float(nblk)), axis=0, keepdims=True)
                hit = blkf == idx
                sel = jnp.where(hit & (r < qb), 1.0, sel)
                g = jnp.where(hit, -jnp.inf, g)
            sel_ref[c] = sel
    m_ref[...] = jnp.full(m_ref.shape, NEG, F32)
    l_ref[...] = jnp.zeros(l_ref.shape, F32)
    acc_ref[...] = jnp.zeros(acc_ref.shape, F32)

    def update(c, hp, jt, bias_row, bias_tile):
        sl = slice(hp * LANES, (hp + 1) * LANES)
        off = pl.multiple_of(jt * TILE, TILE)
        s = lax.dot_general(k_ref[0, pl.ds(off, TILE), sl], qs_ref[c], _NT, preferred_element_type=F32)
        if bias_tile is not None:
            s = s + bias_tile
        mj = jnp.max(s, axis=0, keepdims=True)
        m_old = m_ref[c]
        if bias_row is None:
            m_new = jnp.maximum(m_old, mj)
            p = jnp.exp2(s - m_new)
        else:
            m_new = jnp.maximum(m_old, jnp.where(bias_row, mj, NEG))
            p = jnp.exp2(s + jnp.where(bias_row, -m_new, NEG))
        alpha = jnp.exp2(m_old - m_new)
        m_ref[c] = m_new
        l_ref[c] = alpha * l_ref[c] + jnp.sum(p, axis=0, keepdims=True)
        pv = jnp.dot(vt_ref[0, jt, sl, :], p.astype(BF16), preferred_element_type=F32)
        acc_ref[c] = alpha * acc_ref[c] + jnp.concatenate([pv[:HEAD_DIM, :TILE], pv[HEAD_DIM:, TILE:]], axis=1)

    def past_blocks(j0, nb):
        for u in range(nb * N_QSUB):
            j = j0 + u // N_QSUB
            for c in range(N_CHAINS):
                update(c, c // N_QSUB, j0 * N_QSUB + u, sel_ref[c, pl.ds(j, 1), :] > 0.5, None)

    def past(jj, carry):
        past_blocks(jj * PAST_BLOCKS, PAST_BLOCKS)
        return carry

    lax.fori_loop(0, qb // PAST_BLOCKS, past, 0)
    rem = qb % PAST_BLOCKS
    nb = PAST_BLOCKS // 2
    while nb >= 1:
        @pl.when(rem & nb != 0)
        def _(nb=nb):
            past_blocks(qb - (rem & (2 * nb - 1)), nb)
        nb //= 2

    for c in range(N_CHAINS):
        hp, qs = c // N_QSUB, c % N_QSUB
        for kt in range(qs + 1):
            update(c, hp, qb * N_QSUB + kt, None, causal if kt == qs else None)

    for c in range(N_CHAINS):
        hp, qs = c // N_QSUB, c % N_QSUB
        o = acc_ref[c] * (1.0 / l_ref[c])
        o_t = jnp.concatenate([o[:, :TILE], o[:, TILE:]], axis=0)
        o_ref[0, qs * TILE:(qs + 1) * TILE, hp * LANES:(hp + 1) * LANES] = o_t.T.astype(BF16)


def _moba_attention(qk, vt, kmean):
    bsz, seq, _ = qk.shape
    nblk = seq // MOBA_BLOCK
    return pl.pallas_call(
        functools.partial(_moba_kernel, nblk=nblk),
        grid=(bsz, nblk),
        in_specs=[
            pl.BlockSpec((1, MOBA_BLOCK, D_MODEL), lambda b, i: (b, i, 0)),
            pl.BlockSpec((1, seq, D_MODEL), lambda b, i: (b, 0, 1)),
            pl.BlockSpec((1, seq // TILE, D_MODEL, TILE), lambda b, i: (b, 0, 0, 0)),
            pl.BlockSpec((1, nblk, D_MODEL), lambda b, i: (b, 0, 0)),
        ],
        out_specs=pl.BlockSpec((1, MOBA_BLOCK, D_MODEL), lambda b, i: (b, i, 0)),
        out_shape=jax.ShapeDtypeStruct((bsz, seq, D_MODEL), BF16),
        scratch_shapes=[pltpu.VMEM((N_CHAINS, 2 * TILE, LANES), BF16),
                        pltpu.VMEM((N_CHAINS, nblk, 2 * TILE), F32),
                        pltpu.VMEM((N_CHAINS, 1, 2 * TILE), F32),
                        pltpu.VMEM((N_CHAINS, 1, 2 * TILE), F32),
                        pltpu.VMEM((N_CHAINS, HEAD_DIM, 2 * TILE), F32)],
        compiler_params=_params("parallel", "arbitrary"),
        name="moba_attn",
    )(qk, qk, vt, kmean)


def _layer_norm(y, g, b):
    mu = jnp.mean(y, axis=-1, keepdims=True)
    d = y - mu
    var = jnp.mean(d * d, axis=-1, keepdims=True)
    return d * lax.rsqrt(var + LN_EPS) * g + b


def _oproj_ln_kernel(*refs, dils, tm):
    n = len(dils)
    o_refs, refs = refs[:n], refs[n:]
    if n > 1:
        lse_refs, e_ref, refs = refs[:n], refs[n], refs[n + 1:]
    w_ref, b_ref, x_ref, g_ref, beta_ref, out_ref, outb_ref = refs[:7]
    scr = refs[7:]
    if n == 1:
        o = o_refs[0][0, 0]
    else:
        o_scr, lse_scr = scr
        lses = []
        for gi, d in enumerate(dils):
            if d == 1:
                lses.append(lse_refs[gi][0, 0])
            else:
                for ph in range(d):
                    lse_scr.at[gi][pl.ds(ph, tm // d, stride=d), :] = lse_refs[gi][0, ph]
                lses.append(lse_scr[gi])
        mx = functools.reduce(jnp.maximum, lses)
        es = [jnp.exp(l - mx) for l in lses]
        inv = 1.0 / functools.reduce(lambda a, c: a + c, es)
        chunks = [None] * N_CHUNKS
        for gi, d in enumerate(dils):
            wgt = es[gi] * inv
            hi = wgt.astype(BF16)
            lo = (wgt - hi.astype(F32)).astype(BF16)
            wide = (jnp.dot(hi, e_ref[...], preferred_element_type=F32)
                    + jnp.dot(lo, e_ref[...], preferred_element_type=F32))
            for c in range(N_CHUNKS):
                sl = slice(c * LANES, (c + 1) * LANES)
                if d == 1:
                    og = o_refs[gi][0, 0, :, sl].astype(F32)
                else:
                    for ph in range(d):
                        o_scr.at[gi, c][pl.ds(ph, tm // d, stride=d), :] = o_refs[gi][0, ph, :, sl].astype(F32)
                    og = o_scr[gi, c]
                term = wide[:, sl] * og
                chunks[c] = term if chunks[c] is None else chunks[c] + term
        o = jnp.concatenate(chunks, axis=1).astype(BF16)
    h = jnp.dot(o, w_ref[...], preferred_element_type=F32) + b_ref[...]
    y = _layer_norm(DN_ALPHA * x_ref[0] + h, g_ref[...], beta_ref[...])
    out_ref[0] = y
    outb_ref[0] = y.astype(BF16)


def _oproj_ln(os_, lses, dils, w, b, x, g, beta, *, tm=512):
    bsz, seq, _ = x.shape
    tm = min(tm, seq)
    n = len(os_)
    tok = lambda bi, i: (bi, i, 0)
    ph = lambda bi, i: (bi, 0, i, 0)
    fix = lambda bi, i: (0, 0)
    in_specs = [pl.BlockSpec((1, d, tm // d, D_MODEL), ph) for d in dils]
    args = list(os_)
    scratch = []
    if n > 1:
        in_specs += [pl.BlockSpec((1, d, tm // d, LANES), ph) for d in dils]
        args += list(lses)
        expand = np.zeros((LANES, D_MODEL), np.float32)
        for hd in range(N_HEADS):
            expand[hd, hd * HEAD_DIM:(hd + 1) * HEAD_DIM] = 1.0
        in_specs.append(pl.BlockSpec((LANES, D_MODEL), fix))
        args.append(jnp.asarray(expand, BF16))
        scratch = [pltpu.VMEM((n, N_CHUNKS, tm, LANES), F32), pltpu.VMEM((n, tm, LANES), F32)]
    in_specs += [pl.BlockSpec((D_MODEL, D_MODEL), fix), pl.BlockSpec((1, D_MODEL), fix),
                 pl.BlockSpec((1, tm, D_MODEL), tok), pl.BlockSpec((1, D_MODEL), fix), pl.BlockSpec((1, D_MODEL), fix)]
    args += [w, b, x, g, beta]
    return pl.pallas_call(
        functools.partial(_oproj_ln_kernel, dils=tuple(dils), tm=tm),
        grid=(bsz, seq // tm),
        in_specs=in_specs,
        out_specs=[pl.BlockSpec((1, tm, D_MODEL), tok), pl.BlockSpec((1, tm, D_MODEL), tok)],
        out_shape=[jax.ShapeDtypeStruct((bsz, seq, D_MODEL), F32), jax.ShapeDtypeStruct((bsz, seq, D_MODEL), BF16)],
        scratch_shapes=scratch,
        compiler_params=_params("parallel", "parallel"),
        name="oproj_ln",
    )(*args)


FFN_ROWS = 256


def _swiglu_ln(x, xb, wg_ref, wu_ref, wd_ref, g_ref, beta_ref):
    gate = jnp.dot(xb, wg_ref[...], preferred_element_type=F32)
    up = jnp.dot(xb, wu_ref[...], preferred_element_type=F32)
    hid = (gate * jax.nn.sigmoid(gate) * up).astype(BF16)
    y = jnp.dot(hid, wd_ref[...], preferred_element_type=F32)
    return _layer_norm(DN_ALPHA * x + y, g_ref[...], beta_ref[...])


def _ffn_ln_kernel(x_ref, xb_ref, wg_ref, wu_ref, wd_ref, g_ref, beta_ref, out_ref, outb_ref):
    for r in range(x_ref.shape[0] // FFN_ROWS):
        rows = slice(r * FFN_ROWS, (r + 1) * FFN_ROWS)
        y = _swiglu_ln(x_ref[rows, :], xb_ref[rows, :], wg_ref, wu_ref, wd_ref, g_ref, beta_ref)
        out_ref[rows, :] = y
        outb_ref[rows, :] = y.astype(BF16)


def _oproj_ffn_kernel(o_ref, wo_ref, bo_ref, x_ref, g0_ref, beta0_ref, wg_ref, wu_ref, wd_ref, g1_ref, beta1_ref,
                      out_ref, outb_ref):
    for r in range(x_ref.shape[0] // FFN_ROWS):
        rows = slice(r * FFN_ROWS, (r + 1) * FFN_ROWS)
        h = jnp.dot(o_ref[rows, :], wo_ref[...], preferred_element_type=F32) + bo_ref[...]
        x1 = _layer_norm(DN_ALPHA * x_ref[rows, :] + h, g0_ref[...], beta0_ref[...])
        y = _swiglu_ln(x1, x1.astype(BF16), wg_ref, wu_ref, wd_ref, g1_ref, beta1_ref)
        out_ref[rows, :] = y
        outb_ref[rows, :] = y.astype(BF16)


def _oproj_ffn(o, wo, bo, x, g0, beta0, wg, wu, wd, g1, beta1, *, tm=512):
    t = x.shape[0]
    tm = min(tm, t)
    row = lambda i: (i, 0)
    fix = lambda i: (0, 0)
    resident = dict(pipeline_mode=pl.Buffered(1))
    vec = pl.BlockSpec((1, D_MODEL), fix)
    return pl.pallas_call(
        _oproj_ffn_kernel,
        grid=(t // tm,),
        in_specs=[pl.BlockSpec((tm, D_MODEL), row), pl.BlockSpec((D_MODEL, D_MODEL), fix, **resident), vec,
                  pl.BlockSpec((tm, D_MODEL), row), vec, vec,
                  pl.BlockSpec((D_MODEL, D_FF), fix, **resident),
                  pl.BlockSpec((D_MODEL, D_FF), fix, **resident),
                  pl.BlockSpec((D_FF, D_MODEL), fix, **resident), vec, vec],
        out_specs=[pl.BlockSpec((tm, D_MODEL), row), pl.BlockSpec((tm, D_MODEL), row)],
        out_shape=[jax.ShapeDtypeStruct((t, D_MODEL), F32), jax.ShapeDtypeStruct((t, D_MODEL), BF16)],
        compiler_params=_params("parallel"),
        name="oproj_ffn",
    )(o, wo, bo, x, g0, beta0, wg, wu, wd, g1, beta1)


def _ffn_ln(x, xb, wg, wu, wd, g, beta, *, tm=1024):
    t = x.shape[0]
    tm = min(tm, t)
    row = lambda i: (i, 0)
    fix = lambda i: (0, 0)
    resident = dict(pipeline_mode=pl.Buffered(1))
    return pl.pallas_call(
        _ffn_ln_kernel,
        grid=(t // tm,),
        in_specs=[pl.BlockSpec((tm, D_MODEL), row), pl.BlockSpec((tm, D_MODEL), row),
                  pl.BlockSpec((D_MODEL, D_FF), fix, **resident),
                  pl.BlockSpec((D_MODEL, D_FF), fix, **resident),
                  pl.BlockSpec((D_FF, D_MODEL), fix, **resident),
                  pl.BlockSpec((1, D_MODEL), fix), pl.BlockSpec((1, D_MODEL), fix)],
        out_specs=[pl.BlockSpec((tm, D_MODEL), row), pl.BlockSpec((tm, D_MODEL), row)],
        out_shape=[jax.ShapeDtypeStruct((t, D_MODEL), F32), jax.ShapeDtypeStruct((t, D_MODEL), BF16)],
        compiler_params=_params("parallel"),
        name="ffn_ln",
    )(x, xb, wg, wu, wd, g, beta)


def kernel(x, positions, ln_g, ln_b, a_w_qkv, a_b_qkv, a_sinks, a_w_o, a_b_o, b_w_qkv, b_w_o, c_w_qkv, c_w_o,
           w_gate_up, w_down):
    bsz, seq, d = x.shape
    assert d == D_MODEL and seq % (B_GROUPS[-1][1] * TILE) == 0 and seq % MOBA_BLOCK == 0
    t = bsz * seq
    tables = _rope_tables(positions.reshape(t, 1))
    xf = xb = x
    zero_bias = jnp.zeros((1, D_MODEL), F32)
    zero_bias3 = jnp.zeros((1, 3 * D_MODEL), F32)
    row = lambda v: v.reshape(1, -1)
    for i in range(DEPTH):
        kind, j = i % 3, i // 3
        g0, b0, g1, b1 = row(ln_g[i, 0]), row(ln_b[i, 0]), row(ln_g[i, 1]), row(ln_b[i, 1])
        wgu = w_gate_up[i]
        ffn_w = (wgu[:, :D_FF].astype(BF16), wgu[:, D_FF:].astype(BF16), w_down[i].astype(BF16))
        if kind == 1:
            w = b_w_qkv[j].astype(BF16)
            os_, lses, dils = [], [], []
            for gi, (window, dil) in enumerate(B_GROUPS):
                cols = slice(3 * gi * D_MODEL, 3 * (gi + 1) * D_MODEL)
                ln = seq // dil
                if dil == 1:
                    qk, vt = _qkv_proj(xb, w[:, cols], zero_bias3, tables)
                    qk = qk[:, 0]
                else:
                    qk, vt = _qkv_proj(xb, w[:, cols], zero_bias3, tables, dil=dil)[0], None
                    qk = qk.reshape(bsz * dil, ln, 3 * D_MODEL)
                o, lse = _band_attention(qk, vt, n_back=window // dil, with_lse=True)
                os_.append(o.reshape(bsz, dil, ln, D_MODEL))
                lses.append(lse.reshape(bsz, dil, ln, LANES))
                dils.append(dil)
            xf, xb = _oproj_ln(os_, lses, dils, b_w_o[j].astype(BF16), zero_bias, xf, g0, b0)
            y, yb = _ffn_ln(xf.reshape(t, d), xb.reshape(t, d), *ffn_w, g1, b1)
        else:
            if kind == 0:
                qk, vt = _qkv_proj(xb, a_w_qkv[j].astype(BF16), row(a_b_qkv[j]), tables)
                o = _band_attention(qk[:, 0], vt, n_back=A_WINDOW - 1, sinks=a_sinks[j])[0]
                wo, bo = a_w_o[j], row(a_b_o[j])
            else:
                qk, vt, kmean = _qkv_proj(xb, c_w_qkv[j].astype(BF16), zero_bias3, tables, with_kmean=True)
                o = _moba_attention(qk[:, 0], vt, kmean[:, :, 0])
                wo, bo = c_w_o[j], zero_bias
            y, yb = _oproj_ffn(o.reshape(t, d), wo.astype(BF16), bo, xf.reshape(t, d), g0, b0, *ffn_w, g1, b1)
        xf, xb = y.reshape(bsz, seq, d), yb.reshape(bsz, seq, d)
    return xf
```

```python
import functools
import math

import numpy as np
import jax
import jax.numpy as jnp
from jax import lax
from jax.experimental import pallas as pl
from jax.experimental.pallas import tpu as pltpu

D_MODEL = 1024
N_HEADS = 16
HEAD_DIM = 64
ROT_DIM = HEAD_DIM // 4
ROPE_THETA = 500000.0
A_KV_HEADS = 4
A_WINDOW = 128
B_GROUPS = ((128, 1), (512, 4), (2048, 16))
MOBA_BLOCK = 256
MOBA_TOPK = 3
D_FF = 2816
DEPTH = 4
DN_ALPHA = (2 * DEPTH) ** 0.25
LN_EPS = 1e-5
NEG = -1e30

LANES = 128
TILE = 128
HEAD_PAIRS = N_HEADS // 2
N_CHUNKS = D_MODEL // LANES
VMEM_LIMIT = 56 * 1024 * 1024
LOG2E = math.log2(math.e)
LN2 = math.log(2.0)
Q_SCALE = HEAD_DIM ** -0.5 * LOG2E

F32 = jnp.float32
BF16 = jnp.bfloat16
_NT = (((1,), (1,)), ((), ()))
_TN = (((0,), (0,)), ((), ()))


def _params(*sem):
    return pltpu.CompilerParams(dimension_semantics=sem, vmem_limit_bytes=VMEM_LIMIT)


_INV_FREQ = [float(np.float32(ROPE_THETA ** (-(2.0 * i) / ROT_DIM))) for i in range(ROT_DIM // 2)]


def _rope_table_kernel(pos_ref, cos_ref, sp_ref, sm_ref):
    pos = pos_ref[...].astype(F32)
    lane = lax.broadcasted_iota(jnp.int32, (1, LANES), 1)
    c = lane & (HEAD_DIM - 1)
    f = c & (ROT_DIM // 2 - 1)
    inv = jnp.zeros((1, LANES), F32)
    for i, v in enumerate(_INV_FREQ):
        inv = jnp.where(f == i, v, inv)
    ang = pos * inv
    cs, sn = jnp.cos(ang), jnp.sin(ang)
    cos_ref[...] = jnp.where(c < ROT_DIM, cs, 1.0)
    sp_ref[...] = jnp.where((c >= ROT_DIM // 2) & (c < ROT_DIM), sn, 0.0)
    sm_ref[...] = jnp.where(c < ROT_DIM // 2, -sn, 0.0)


def _rope_tables(pos_col):
    t = pos_col.shape[0]
    tm = min(t, 2048)
    spec = pl.BlockSpec((tm, LANES), lambda i: (i, 0))
    shp = jax.ShapeDtypeStruct((t, LANES), F32)
    return pl.pallas_call(
        _rope_table_kernel,
        grid=(t // tm,),
        in_specs=[pl.BlockSpec((tm, 1), lambda i: (i, 0))],
        out_specs=[spec, spec, spec],
        out_shape=[shp, shp, shp],
        compiler_params=_params("parallel"),
        name="rope_tables",
    )(pos_col)


PROJ_ROWS = 256


def _rotary(y, cs, sp, sm):
    return y * cs + pltpu.roll(y, ROT_DIM // 2, 1) * sp + pltpu.roll(y, LANES - ROT_DIM // 2, 1) * sm


def _proj_shared_kv(xr, w_ref, b_ref, tabs, qk_ref, vt_ref, kind, r):
    kv = A_KV_HEADS * HEAD_DIM
    base = D_MODEL + (kind - 1) * kv
    acc = jnp.dot(xr, w_ref[:, base:base + kv], preferred_element_type=F32) + b_ref[:, base:base + kv]
    lane = lax.broadcasted_iota(jnp.int32, (PROJ_ROWS, LANES), 1)
    rows = slice(r * PROJ_ROWS, (r + 1) * PROJ_ROWS)
    pairs_per_kv = N_HEADS // A_KV_HEADS // 2
    for cc in range(kv // LANES):
        y = acc[:, cc * LANES:(cc + 1) * LANES]
        if kind == 1:
            y = _rotary(y, *tabs)
        swapped = pltpu.roll(y, HEAD_DIM, 1)
        for par, both in enumerate((jnp.where(lane < HEAD_DIM, y, swapped), jnp.where(lane < HEAD_DIM, swapped, y))):
            g = 2 * cc + par
            if kind == 2:
                tiles = [both[tl * TILE:(tl + 1) * TILE].T.astype(BF16) for tl in range(PROJ_ROWS // TILE)]
            for hp in range(g * pairs_per_kv, (g + 1) * pairs_per_kv):
                sl = slice(hp * LANES, (hp + 1) * LANES)
                if kind == 1:
                    qk_ref[0, 0, rows, D_MODEL + hp * LANES:D_MODEL + (hp + 1) * LANES] = both.astype(BF16)
                else:
                    for tl, tile in enumerate(tiles):
                        vt_ref[0, r * (PROJ_ROWS // TILE) + tl, sl, :] = tile


def _proj_kernel(x_ref, w_ref, b_ref, cos_ref, sp_ref, sm_ref, *rest, tm, dil, with_kmean, gqa):
    rest = list(rest)
    perm_ref = rest.pop(0) if dil > 1 else None
    qk_ref = rest.pop(0)
    vt_ref = rest.pop(0) if dil == 1 else None
    km_ref = rest.pop(0) if with_kmean else None
    sub = PROJ_ROWS // dil
    for r in range(tm // PROJ_ROWS):
        rows = slice(r * PROJ_ROWS, (r + 1) * PROJ_ROWS)
        xr = x_ref[0, rows, :].astype(BF16)
        if dil == 1:
            tabs = (cos_ref[0, rows, :], sp_ref[0, rows, :], sm_ref[0, rows, :])
        else:
            xr = jnp.dot(perm_ref[...], xr, preferred_element_type=F32).astype(BF16)
            tabs = tuple(
                jnp.concatenate([tb.at[0][pl.ds(r * PROJ_ROWS + ph, sub, stride=dil), :] for ph in range(dil)], axis=0)
                for tb in (cos_ref, sp_ref, sm_ref))
        for kind in range(3):
            if gqa and kind > 0:
                _proj_shared_kv(xr, w_ref, b_ref, tabs, qk_ref, vt_ref, kind, r)
                continue
            cols = slice(kind * D_MODEL, (kind + 1) * D_MODEL)
            acc = jnp.dot(xr, w_ref[:, cols], preferred_element_type=F32) + b_ref[:, cols]
            cs, sp, sm = [tb * Q_SCALE for tb in tabs] if kind == 0 else tabs
            for c in range(N_CHUNKS):
                sl = slice(c * LANES, (c + 1) * LANES)
                osl = slice(kind * D_MODEL + c * LANES, kind * D_MODEL + (c + 1) * LANES)
                y = acc[:, sl]
                if kind < 2:
                    y = _rotary(y, cs, sp, sm)
                if dil > 1:
                    yb = y.astype(BF16)
                    for ph in range(dil):
                        qk_ref[0, ph, r * sub:(r + 1) * sub, osl] = yb[ph * sub:(ph + 1) * sub]
                elif kind < 2:
                    qk_ref[0, 0, rows, osl] = y.astype(BF16)
                else:
                    for tl in range(PROJ_ROWS // TILE):
                        vt_ref[0, r * (PROJ_ROWS // TILE) + tl, sl, :] = y[tl * TILE:(tl + 1) * TILE].T.astype(BF16)
                if with_kmean and kind == 1:
                    km_ref[0, r, :, sl] = jnp.sum(y, axis=0, keepdims=True) * (1.0 / MOBA_BLOCK)


def _qkv_proj(xb, w, b, tables, *, dil=1, with_kmean=False, tm=512):
    bsz, seq, _ = xb.shape
    tm = min(tm, seq)
    gqa = w.shape[1] == D_MODEL + 2 * A_KV_HEADS * HEAD_DIM
    assert gqa or w.shape[1] == 3 * D_MODEL
    assert not gqa or (dil == 1 and not with_kmean)
    assert PROJ_ROWS == MOBA_BLOCK and tm % PROJ_ROWS == 0 and PROJ_ROWS % (16 * dil) == 0
    tabs = [tb.reshape(bsz, seq, LANES) for tb in tables]
    tab_spec = pl.BlockSpec((1, tm, LANES), lambda bi, i: (bi, i, 0))
    fix = lambda bi, i: (0, 0)
    once = dict(pipeline_mode=pl.Buffered(1))
    ncol = 2 * D_MODEL if dil == 1 else 3 * D_MODEL
    out_specs = [pl.BlockSpec((1, dil, tm // dil, ncol), lambda bi, i: (bi, 0, i, 0))]
    out_shape = [jax.ShapeDtypeStruct((bsz, dil, seq // dil, ncol), BF16)]
    if dil == 1:
        out_specs.append(pl.BlockSpec((1, tm // TILE, D_MODEL, TILE), lambda bi, i: (bi, i, 0, 0)))
        out_shape.append(jax.ShapeDtypeStruct((bsz, seq // TILE, D_MODEL, TILE), BF16))
    if with_kmean:
        out_specs.append(pl.BlockSpec((1, tm // MOBA_BLOCK, 1, D_MODEL), lambda bi, i: (bi, i, 0, 0)))
        out_shape.append(jax.ShapeDtypeStruct((bsz, seq // MOBA_BLOCK, 1, D_MODEL), F32))
    in_specs = [
        pl.BlockSpec((1, tm, D_MODEL), lambda bi, i: (bi, i, 0)),
        pl.BlockSpec((D_MODEL, w.shape[1]), fix, **once),
        pl.BlockSpec((1, w.shape[1]), fix),
        tab_spec, tab_spec, tab_spec,
    ]
    args = [xb, w, b, *tabs]
    if dil > 1:
        p = np.arange(PROJ_ROWS)
        perm = np.zeros((PROJ_ROWS, PROJ_ROWS), np.float32)
        perm[p, (p % (PROJ_ROWS // dil)) * dil + p // (PROJ_ROWS // dil)] = 1.0
        in_specs.append(pl.BlockSpec((PROJ_ROWS, PROJ_ROWS), fix))
        args.append(jnp.asarray(perm, BF16))
    return pl.pallas_call(
        functools.partial(_proj_kernel, tm=tm, dil=dil, with_kmean=with_kmean, gqa=gqa),
        grid=(bsz, seq // tm),
        in_specs=in_specs,
        out_specs=out_specs,
        out_shape=out_shape,
        compiler_params=_params("parallel", "parallel"),
        name="qkv_proj",
    )(*args)


def _band_bias(n_back):
    k = np.arange(TILE)[:, None]
    q = (np.arange(2 * TILE) % TILE)[None, :]
    prev = q + TILE - k <= n_back
    own = k <= q
    return jnp.asarray(np.where(np.stack([prev, np.zeros_like(prev), own]), 0.0, NEG), F32)


def _stack_heads(q):
    lane = lax.broadcasted_iota(jnp.int32, q.shape, 1)
    zero = jnp.zeros_like(q)
    return jnp.concatenate([jnp.where(lane < HEAD_DIM, q, zero), jnp.where(lane >= HEAD_DIM, q, zero)], axis=0)


BAND_QBLOCKS = 8


def _band_attn_kernel(*refs, with_sink, with_lse, v_transposed, nq, ns, merged):
    refs = list(refs)
    sink_ref = refs.pop(0) if with_sink else None
    q_ref, kp_ref, kc_ref, vp_ref, vc_ref, bias_ref, o_ref = refs[:7]
    lse_ref = refs[7] if with_lse else None
    first = (pl.program_id(1) == 0).astype(jnp.int32)
    col = lax.broadcasted_iota(jnp.int32, (1, 2 * TILE), 1)
    own_f32 = (lax.broadcasted_iota(jnp.int32, (TILE, 2 * TILE), 0)
               <= (lax.broadcasted_iota(jnp.int32, (TILE, 2 * TILE), 1) & (TILE - 1)))
    for sq, t in [(sq, t) for sq in range(ns) for t in range(nq)]:
        rows = slice(t * TILE, (t + 1) * TILE)
        prow = slice((t - 1) * TILE, t * TILE)
        lse_rows = []
        for hp in range(HEAD_PAIRS):
            sl = slice(hp * LANES, (hp + 1) * LANES)
            qs = _stack_heads(q_ref[sq, rows, sl])
            if t == 0:
                k_prev, bias_prev = kp_ref[sq, :, sl], bias_ref[first]
                v_prev = vp_ref[sq, 0, sl, :] if v_transposed else vp_ref[sq, :, sl]
            else:
                k_prev, bias_prev = kc_ref[sq, prow, sl], bias_ref[0]
                v_prev = vc_ref[sq, t - 1, sl, :] if v_transposed else vc_ref[sq, prow, sl]
            v_own = vc_ref[sq, t, sl, :] if v_transposed else vc_ref[sq, rows, sl]
            if merged:
                s_prev = lax.dot_general(k_prev, qs, _NT, preferred_element_type=F32)
                if t == 0:
                    s_prev = s_prev + jnp.where(first == 1, NEG, 0.0)
                s_own = lax.dot_general(kc_ref[sq, rows, sl], qs, _NT, preferred_element_type=F32)
                s = jnp.where(own_f32, s_own, s_prev)
                m = jnp.max(s, axis=0, keepdims=True)
                if with_sink:
                    sk = jnp.where(col < TILE, sink_ref[2 * hp], sink_ref[2 * hp + 1]) * LOG2E
                    m = jnp.maximum(m, sk)
                p = jnp.exp2(s - m)
                l = jnp.sum(p, axis=0, keepdims=True)
                if with_sink:
                    l = l + jnp.exp2(sk - m)
                p_own = jnp.where(own_f32, p, 0.0).astype(BF16)
                p_prev = jnp.where(own_f32, 0.0, p).astype(BF16)
                if v_transposed:
                    pv = (jnp.dot(v_own, p_own, preferred_element_type=F32)
                          + jnp.dot(v_prev, p_prev, preferred_element_type=F32))
                else:
                    pv = (lax.dot_general(v_own, p_own, _TN, preferred_element_type=F32)
                          + lax.dot_general(v_prev, p_prev, _TN, preferred_element_type=F32))
                inv = 1.0 / l
                o_t = jnp.concatenate([pv[:HEAD_DIM, :TILE] * inv[:, :TILE], pv[HEAD_DIM:, TILE:] * inv[:, TILE:]],
                                      axis=0)
                o_ref[sq, rows, sl] = o_t.T.astype(BF16)
                if with_lse:
                    lse = (m + jnp.log2(l)) * LN2
                    lse_rows += [lse[:, :TILE], lse[:, TILE:]]
                continue
            parts = []
            for k, v, bias in ((k_prev, v_prev, bias_prev), (kc_ref[sq, rows, sl], v_own, bias_ref[2])):
                s = lax.dot_general(k, qs, _NT, preferred_element_type=F32) + bias
                m = jnp.max(s, axis=0, keepdims=True)
                p = jnp.exp2(s - m)
                l = jnp.sum(p, axis=0, keepdims=True)
                if v_transposed:
                    pv = jnp.dot(v, p.astype(BF16), preferred_element_type=F32)
                else:
                    pv = lax.dot_general(v, p.astype(BF16), _TN, preferred_element_type=F32)
                parts.append((m, l, pv))
            (m0, l0, pv0), (m1, l1, pv1) = parts
            m = jnp.maximum(m0, m1)
            if with_sink:
                sk = jnp.where(col < TILE, sink_ref[2 * hp], sink_ref[2 * hp + 1]) * LOG2E
                m = jnp.maximum(m, sk)
            a0, a1 = jnp.exp2(m0 - m), jnp.exp2(m1 - m)
            l = a0 * l0 + a1 * l1
            if with_sink:
                l = l + jnp.exp2(sk - m)
            inv = 1.0 / l
            w0, w1 = a0 * inv, a1 * inv
            o_t = jnp.concatenate(
                [pv0[:HEAD_DIM, :TILE] * w0[:, :TILE] + pv1[:HEAD_DIM, :TILE] * w1[:, :TILE],
                 pv0[HEAD_DIM:, TILE:] * w0[:, TILE:] + pv1[HEAD_DIM:, TILE:] * w1[:, TILE:]], axis=0)
            o_ref[sq, rows, sl] = o_t.T.astype(BF16)
            if with_lse:
                lse = (m + jnp.log2(l)) * LN2
                lse_rows += [lse[:, :TILE], lse[:, TILE:]]
        if with_lse:
            pad = jnp.zeros((LANES - N_HEADS, TILE), F32)
            lse_ref[sq, rows, :] = jnp.concatenate(lse_rows + [pad], axis=0).T


def _band_attention(qk, vt, *, n_back, sinks=None, with_lse=False):
    nseq, ln, _ = qk.shape
    nq = math.gcd(BAND_QBLOCKS, ln // TILE)
    ns = math.gcd(BAND_QBLOCKS // nq, nseq)
    blk = (ns, nq * TILE, D_MODEL)
    cur = lambda c: pl.BlockSpec(blk, lambda s, i: (s, i, c))
    prev = lambda c: pl.BlockSpec((ns, TILE, D_MODEL), lambda s, i: (s, jnp.maximum(i * nq - 1, 0), c))
    if vt is None:
        v_specs, v_args = [prev(2), cur(2)], [qk, qk]
    else:
        v_specs = [pl.BlockSpec((ns, 1, D_MODEL, TILE), lambda s, i: (s, jnp.maximum(i * nq - 1, 0), 0, 0)),
                   pl.BlockSpec((ns, nq, D_MODEL, TILE), lambda s, i: (s, i, 0, 0))]
        v_args = [vt, vt]
    in_specs = [cur(0), prev(1), cur(1)] + v_specs + [pl.BlockSpec((3, TILE, 2 * TILE), lambda s, i: (0, 0, 0))]
    args = [qk] * 3 + v_args + [_band_bias(n_back)]
    if sinks is not None:
        in_specs.insert(0, pl.BlockSpec(memory_space=pltpu.SMEM))
        args.insert(0, sinks)
    out_specs = [pl.BlockSpec(blk, lambda s, i: (s, i, 0))]
    out_shape = [jax.ShapeDtypeStruct((nseq, ln, D_MODEL), BF16)]
    if with_lse:
        out_specs.append(pl.BlockSpec((ns, nq * TILE, LANES), lambda s, i: (s, i, 0)))
        out_shape.append(jax.ShapeDtypeStruct((nseq, ln, LANES), F32))
    return pl.pallas_call(
        functools.partial(_band_attn_kernel, with_sink=sinks is not None, with_lse=with_lse,
                          v_transposed=vt is not None, nq=nq, ns=ns, merged=n_back == TILE - 1),
        grid=(nseq // ns, ln // (nq * TILE)),
        in_specs=in_specs,
        out_specs=out_specs,
        out_shape=out_shape,
        compiler_params=_params("parallel", "arbitrary"),
        name="band_attn",
    )(*args)


N_QSUB = MOBA_BLOCK // TILE
N_CHAINS = HEAD_PAIRS * N_QSUB
PAST_BLOCKS = 4


def _moba_kernel(q_ref, k_ref, vt_ref, km_ref, o_ref, qs_ref, sel_ref, m_ref, l_ref, acc_ref, *, nblk):
    qb = pl.program_id(1)
    blk = lax.broadcasted_iota(jnp.int32, (nblk, MOBA_BLOCK), 0)
    blkf = blk.astype(F32)
    lane = lax.broadcasted_iota(jnp.int32, (MOBA_BLOCK, LANES), 1)
    kidx = lax.broadcasted_iota(jnp.int32, (TILE, 2 * TILE), 0)
    qidx = lax.broadcasted_iota(jnp.int32, (TILE, 2 * TILE), 1) & (TILE - 1)
    causal = jnp.where(kidx <= qidx, 0.0, NEG).astype(F32)

    for hp in range(HEAD_PAIRS):
        sl = slice(hp * LANES, (hp + 1) * LANES)
        km = km_ref[0, :, sl]
        km_hi = km.astype(BF16)
        km_lo = (km - km_hi.astype(F32)).astype(BF16)
        for qs in range(N_QSUB):
            c = hp * N_QSUB + qs
            stacked = _stack_heads(q_ref[0, qs * TILE:(qs + 1) * TILE, sl])
            qs_ref[c] = stacked
            gate = (lax.dot_general(km_hi, stacked, _NT, preferred_element_type=F32)
                    + lax.dot_general(km_lo, stacked, _NT, preferred_element_type=F32))
            g = jnp.where(blk < qb, gate, NEG)
            sel = jnp.zeros((nblk, 2 * TILE), F32)
            for r in range(min(MOBA_TOPK, nblk)):
                mx = jnp.max(g, axis=0, keepdims=True)
                idx = jnp.min(jnp.where(g == mx, blkf, float(nblk)), axis=0, keepdims=True)
                hit = blkf == idx
                sel = jnp.where(hit & (r < qb), 1.0, sel)
                g = jnp.where(hit, -jnp.inf, g)
            sel_ref[c] = sel
    m_ref[...] = jnp.full(m_ref.shape, NEG, F32)
    l_ref[...] = jnp.zeros(l_ref.shape, F32)
    acc_ref[...] = jnp.zeros(acc_ref.shape, F32)

    def update(c, hp, jt, bias_row, bias_tile):
        sl = slice(hp * LANES, (hp + 1) * LANES)
        off = pl.multiple_of(jt * TILE, TILE)
        s = lax.dot_general(k_ref[0, pl.ds(off, TILE), sl], qs_ref[c], _NT, preferred_element_type=F32)
        if bias_tile is not None:
            s = s + bias_tile
        mj = jnp.max(s, axis=0, keepdims=True)
        m_old = m_ref[c]
        if bias_row is None:
            m_new = jnp.maximum(m_old, mj)
            p = jnp.exp2(s - m_new)
        else:
            m_new = jnp.maximum(m_old, jnp.where(bias_row, mj, NEG))
            p = jnp.exp2(s + jnp.where(bias_row, -m_new, NEG))
        alpha = jnp.exp2(m_old - m_new)
        m_ref[c] = m_new
        l_ref[c] = alpha * l_ref[c] + jnp.sum(p, axis=0, keepdims=True)
        pv = jnp.dot(vt_ref[0, jt, sl, :], p.astype(BF16), preferred_element_type=F32)
        acc_ref[c] = alpha * acc_ref[c] + jnp.concatenate([pv[:HEAD_DIM, :TILE], pv[HEAD_DIM:, TILE:]], axis=1)

    def past_blocks(j0, nb):
        for u in range(nb * N_QSUB):
            j = j0 + u // N_QSUB
            for c in range(N_CHAINS):
                update(c, c // N_QSUB, j0 * N_QSUB + u, sel_ref[c, pl.ds(j, 1), :] > 0.5, None)

    def past(jj, carry):
        past_blocks(jj * PAST_BLOCKS, PAST_BLOCKS)
        return carry

    lax.fori_loop(0, qb // PAST_BLOCKS, past, 0)
    rem = qb % PAST_BLOCKS
    nb = PAST_BLOCKS // 2
    while nb >= 1:
        @pl.when(rem & nb != 0)
        def _(nb=nb):
            past_blocks(qb - (rem & (2 * nb - 1)), nb)
        nb //= 2

    for c in range(N_CHAINS):
        hp, qs = c // N_QSUB, c % N_QSUB
        for kt in range(qs + 1):
            update(c, hp, qb * N_QSUB + kt, None, causal if kt == qs else None)

    for c in range(N_CHAINS):
        hp, qs = c // N_QSUB, c % N_QSUB
        o = acc_ref[c] * (1.0 / l_ref[c])
        o_t = jnp.concatenate([o[:, :TILE], o[:, TILE:]], axis=0)
        o_ref[0, qs * TILE:(qs + 1) * TILE, hp * LANES:(hp + 1) * LANES] = o_t.T.astype(BF16)


def _moba_attention(qk, vt, kmean):
    bsz, seq, _ = qk.shape
    nblk = seq // MOBA_BLOCK
    return pl.pallas_call(
        functools.partial(_moba_kernel, nblk=nblk),
        grid=(bsz, nblk),
        in_specs=[
            pl.BlockSpec((1, MOBA_BLOCK, D_MODEL), lambda b, i: (b, i, 0)),
            pl.BlockSpec((1, seq, D_MODEL), lambda b, i: (b, 0, 1)),
            pl.BlockSpec((1, seq // TILE, D_MODEL, TILE), lambda b, i: (b, 0, 0, 0)),
            pl.BlockSpec((1, nblk, D_MODEL), lambda b, i: (b, 0, 0)),
        ],
        out_specs=pl.BlockSpec((1, MOBA_BLOCK, D_MODEL), lambda b, i: (b, i, 0)),
        out_shape=jax.ShapeDtypeStruct((bsz, seq, D_MODEL), BF16),
        scratch_shapes=[pltpu.VMEM((N_CHAINS, 2 * TILE, LANES), BF16),
                        pltpu.VMEM((N_CHAINS, nblk, 2 * TILE), F32),
                        pltpu.VMEM((N_CHAINS, 1, 2 * TILE), F32),
                        pltpu.VMEM((N_CHAINS, 1, 2 * TILE), F32),
                        pltpu.VMEM((N_CHAINS, HEAD_DIM, 2 * TILE), F32)],
        compiler_params=_params("parallel", "arbitrary"),
        name="moba_attn",
    )(qk, qk, vt, kmean)


def _layer_norm(y, g, b):
    mu = jnp.mean(y, axis=-1, keepdims=True)
    d = y - mu
    var = jnp.mean(d * d, axis=-1, keepdims=True)
    return d * lax.rsqrt(var + LN_EPS) * g + b


def _oproj_ln_kernel(*refs, dils, tm):
    n = len(dils)
    o_refs, refs = refs[:n], refs[n:]
    if n > 1:
        lse_refs, e_ref, refs = refs[:n], refs[n], refs[n + 1:]
    w_ref, b_ref, x_ref, g_ref, beta_ref, out_ref, outb_ref = refs[:7]
    scr = refs[7:]
    if n == 1:
        o = o_refs[0][0, 0]
    else:
        o_scr, lse_scr = scr
        lses = []
        for gi, d in enumerate(dils):
            if d == 1:
                lses.append(lse_refs[gi][0, 0])
            else:
                for ph in range(d):
                    lse_scr.at[gi][pl.ds(ph, tm // d, stride=d), :] = lse_refs[gi][0, ph]
                lses.append(lse_scr[gi])
        mx = functools.reduce(jnp.maximum, lses)
        es = [jnp.exp(l - mx) for l in lses]
        inv = 1.0 / functools.reduce(lambda a, c: a + c, es)
        chunks = [None] * N_CHUNKS
        lane = lax.broadcasted_iota(jnp.int32, (1, LANES), 1)
        packed = jnp.zeros((tm, LANES), F32)
        for gi in range(n):
            wgt = es[gi] * inv
            hi = wgt.astype(BF16).astype(F32)
            for off, part in ((2 * gi * N_HEADS, hi), ((2 * gi + 1) * N_HEADS, wgt - hi)):
                moved = pltpu.roll(part, off, 1) if off else part
                packed = jnp.where((lane >= off) & (lane < off + N_HEADS), moved, packed)
        wide_all = jnp.dot(packed.astype(BF16), e_ref[...], preferred_element_type=F32)
        for gi, d in enumerate(dils):
            wide = wide_all[:, gi * D_MODEL:(gi + 1) * D_MODEL]
            for c in range(N_CHUNKS):
                sl = slice(c * LANES, (c + 1) * LANES)
                if d == 1:
                    og = o_refs[gi][0, 0, :, sl].astype(F32)
                else:
                    for ph in range(d):
                        o_scr.at[gi, c][pl.ds(ph, tm // d, stride=d), :] = o_refs[gi][0, ph, :, sl].astype(F32)
                    og = o_scr[gi, c]
                term = wide[:, sl] * og
                chunks[c] = term if chunks[c] is None else chunks[c] + term
        o = jnp.concatenate(chunks, axis=1).astype(BF16)
    h = jnp.dot(o, w_ref[...], preferred_element_type=F32) + b_ref[...]
    y = _layer_norm(DN_ALPHA * x_ref[0] + h, g_ref[...], beta_ref[...])
    out_ref[0] = y
    outb_ref[0] = y.astype(BF16)


def _oproj_ln(os_, lses, dils, w, b, x, g, beta, *, tm=512):
    bsz, seq, _ = x.shape
    tm = min(tm, seq)
    n = len(os_)
    tok = lambda bi, i: (bi, i, 0)
    ph = lambda bi, i: (bi, 0, i, 0)
    fix = lambda bi, i: (0, 0)
    in_specs = [pl.BlockSpec((1, d, tm // d, D_MODEL), ph) for d in dils]
    args = list(os_)
    scratch = []
    if n > 1:
        in_specs += [pl.BlockSpec((1, d, tm // d, LANES), ph) for d in dils]
        args += list(lses)
        assert 2 * n * N_HEADS <= LANES
        expand = np.zeros((LANES, n * D_MODEL), np.float32)
        for gi in range(n):
            for hd in range(N_HEADS):
                cols = slice(gi * D_MODEL + hd * HEAD_DIM, gi * D_MODEL + (hd + 1) * HEAD_DIM)
                expand[2 * gi * N_HEADS + hd, cols] = 1.0
                expand[(2 * gi + 1) * N_HEADS + hd, cols] = 1.0
        in_specs.append(pl.BlockSpec((LANES, n * D_MODEL), fix))
        args.append(jnp.asarray(expand, BF16))
        scratch = [pltpu.VMEM((n, N_CHUNKS, tm, LANES), F32), pltpu.VMEM((n, tm, LANES), F32)]
    in_specs += [pl.BlockSpec((D_MODEL, D_MODEL), fix), pl.BlockSpec((1, D_MODEL), fix),
                 pl.BlockSpec((1, tm, D_MODEL), tok), pl.BlockSpec((1, D_MODEL), fix), pl.BlockSpec((1, D_MODEL), fix)]
    args += [w, b, x, g, beta]
    return pl.pallas_call(
        functools.partial(_oproj_ln_kernel, dils=tuple(dils), tm=tm),
        grid=(bsz, seq // tm),
        in_specs=in_specs,
        out_specs=[pl.BlockSpec((1, tm, D_MODEL), tok), pl.BlockSpec((1, tm, D_MODEL), tok)],
        out_shape=[jax.ShapeDtypeStruct((bsz, seq, D_MODEL), F32), jax.ShapeDtypeStruct((bsz, seq, D_MODEL), BF16)],
        scratch_shapes=scratch,
        compiler_params=_params("parallel", "parallel"),
        name="oproj_ln",
    )(*args)


FFN_ROWS = 256


def _swiglu_ln(x, xb, wg_ref, wu_ref, wd_ref, g_ref, beta_ref):
    gate = jnp.dot(xb, wg_ref[...], preferred_element_type=F32)
    up = jnp.dot(xb, wu_ref[...], preferred_element_type=F32)
    hid = (gate * jax.nn.sigmoid(gate) * up).astype(BF16)
    y = jnp.dot(hid, wd_ref[...], preferred_element_type=F32)
    return _layer_norm(DN_ALPHA * x + y, g_ref[...], beta_ref[...])


def _ffn_ln_kernel(x_ref, xb_ref, wg_ref, wu_ref, wd_ref, g_ref, beta_ref, out_ref, outb_ref):
    for r in range(x_ref.shape[0] // FFN_ROWS):
        rows = slice(r * FFN_ROWS, (r + 1) * FFN_ROWS)
        y = _swiglu_ln(x_ref[rows, :], xb_ref[rows, :], wg_ref, wu_ref, wd_ref, g_ref, beta_ref)
        out_ref[rows, :] = y
        outb_ref[rows, :] = y.astype(BF16)


def _oproj_ffn_kernel(o_ref, wo_ref, bo_ref, x_ref, g0_ref, beta0_ref, wg_ref, wu_ref, wd_ref, g1_ref, beta1_ref,
                      out_ref, outb_ref):
    for r in range(x_ref.shape[0] // FFN_ROWS):
        rows = slice(r * FFN_ROWS, (r + 1) * FFN_ROWS)
        h = jnp.dot(o_ref[rows, :], wo_ref[...], preferred_element_type=F32) + bo_ref[...]
        x1 = _layer_norm(DN_ALPHA * x_ref[rows, :] + h, g0_ref[...], beta0_ref[...])
        y = _swiglu_ln(x1, x1.astype(BF16), wg_ref, wu_ref, wd_ref, g1_ref, beta1_ref)
        out_ref[rows, :] = y
        outb_ref[rows, :] = y.astype(BF16)


def _oproj_ffn(o, wo, bo, x, g0, beta0, wg, wu, wd, g1, beta1, *, tm=512):
    t = x.shape[0]
    tm = min(tm, t)
    row = lambda i: (i, 0)
    fix = lambda i: (0, 0)
    resident = dict(pipeline_mode=pl.Buffered(1))
    vec = pl.BlockSpec((1, D_MODEL), fix)
    return pl.pallas_call(
        _oproj_ffn_kernel,
        grid=(t // tm,),
        in_specs=[pl.BlockSpec((tm, D_MODEL), row), pl.BlockSpec((D_MODEL, D_MODEL), fix, **resident), vec,
                  pl.BlockSpec((tm, D_MODEL), row), vec, vec,
                  pl.BlockSpec((D_MODEL, D_FF), fix, **resident),
                  pl.BlockSpec((D_MODEL, D_FF), fix, **resident),
                  pl.BlockSpec((D_FF, D_MODEL), fix, **resident), vec, vec],
        out_specs=[pl.BlockSpec((tm, D_MODEL), row), pl.BlockSpec((tm, D_MODEL), row)],
        out_shape=[jax.ShapeDtypeStruct((t, D_MODEL), F32), jax.ShapeDtypeStruct((t, D_MODEL), BF16)],
        compiler_params=_params("parallel"),
        name="oproj_ffn",
    )(o, wo, bo, x, g0, beta0, wg, wu, wd, g1, beta1)


def _ffn_ln(x, xb, wg, wu, wd, g, beta, *, tm=1024):
    t = x.shape[0]
    tm = min(tm, t)
    row = lambda i: (i, 0)
    fix = lambda i: (0, 0)
    resident = dict(pipeline_mode=pl.Buffered(1))
    return pl.pallas_call(
        _ffn_ln_kernel,
        grid=(t // tm,),
        in_specs=[pl.BlockSpec((tm, D_MODEL), row), pl.BlockSpec((tm, D_MODEL), row),
                  pl.BlockSpec((D_MODEL, D_FF), fix, **resident),
                  pl.BlockSpec((D_MODEL, D_FF), fix, **resident),
                  pl.BlockSpec((D_FF, D_MODEL), fix, **resident),
                  pl.BlockSpec((1, D_MODEL), fix), pl.BlockSpec((1, D_MODEL), fix)],
        out_specs=[pl.BlockSpec((tm, D_MODEL), row), pl.BlockSpec((tm, D_MODEL), row)],
        out_shape=[jax.ShapeDtypeStruct((t, D_MODEL), F32), jax.ShapeDtypeStruct((t, D_MODEL), BF16)],
        compiler_params=_params("parallel"),
        name="ffn_ln",
    )(x, xb, wg, wu, wd, g, beta)


def kernel(x, positions, ln_g, ln_b, a_w_qkv, a_b_qkv, a_sinks, a_w_o, a_b_o, b_w_qkv, b_w_o, c_w_qkv, c_w_o,
           w_gate_up, w_down):
    bsz, seq, d = x.shape
    assert d == D_MODEL and seq % (B_GROUPS[-1][1] * TILE) == 0 and seq % MOBA_BLOCK == 0
    t = bsz * seq
    tables = _rope_tables(positions.reshape(t, 1))
    xf = xb = x
    zero_bias = jnp.zeros((1, D_MODEL), F32)
    zero_bias3 = jnp.zeros((1, 3 * D_MODEL), F32)
    row = lambda v: v.reshape(1, -1)
    for i in range(DEPTH):
        kind, j = i % 3, i // 3
        g0, b0, g1, b1 = row(ln_g[i, 0]), row(ln_b[i, 0]), row(ln_g[i, 1]), row(ln_b[i, 1])
        wgu = w_gate_up[i]
        ffn_w = (wgu[:, :D_FF].astype(BF16), wgu[:, D_FF:].astype(BF16), w_down[i].astype(BF16))
        if kind == 1:
            w = b_w_qkv[j].astype(BF16)
            os_, lses, dils = [], [], []
            for gi, (window, dil) in enumerate(B_GROUPS):
                cols = slice(3 * gi * D_MODEL, 3 * (gi + 1) * D_MODEL)
                ln = seq // dil
                if dil == 1:
                    qk, vt = _qkv_proj(xb, w[:, cols], zero_bias3, tables)
                    qk = qk[:, 0]
                else:
                    qk, vt = _qkv_proj(xb, w[:, cols], zero_bias3, tables, dil=dil)[0], None
                    qk = qk.reshape(bsz * dil, ln, 3 * D_MODEL)
                o, lse = _band_attention(qk, vt, n_back=window // dil, with_lse=True)
                os_.append(o.reshape(bsz, dil, ln, D_MODEL))
                lses.append(lse.reshape(bsz, dil, ln, LANES))
                dils.append(dil)
            xf, xb = _oproj_ln(os_, lses, dils, b_w_o[j].astype(BF16), zero_bias, xf, g0, b0)
            y, yb = _ffn_ln(xf.reshape(t, d), xb.reshape(t, d), *ffn_w, g1, b1)
        else:
            if kind == 0:
                qk, vt = _qkv_proj(xb, a_w_qkv[j].astype(BF16), row(a_b_qkv[j]), tables)
                o = _band_attention(qk[:, 0], vt, n_back=A_WINDOW - 1, sinks=a_sinks[j])[0]
                wo, bo = a_w_o[j], row(a_b_o[j])
            else:
                qk, vt, kmean = _qkv_proj(xb, c_w_qkv[j].astype(BF16), zero_bias3, tables, with_kmean=True)
                o = _moba_attention(qk[:, 0], vt, kmean[:, :, 0])
                wo, bo = c_w_o[j], zero_bias
            y, yb = _oproj_ffn(o.reshape(t, d), wo.astype(BF16), bo, xf.reshape(t, d), g0, b0, *ffn_w, g1, b1)
        xf, xb = y.reshape(bsz, seq, d), yb.reshape(bsz, seq, d)
    return xf
```
